```python
import jax, jax.numpy as jnp
from jax import lax
import numpy as np

D_MODEL = 2048
BATCH = 4
SEQ = 2048
DEPTH = 1

N_META = 16
CHUNK = 64
MIX_WIDTH = D_MODEL
GLA_WIDTH = MIX_WIDTH // 2
POOL_WIDTH = MIX_WIDTH - GLA_WIDTH
GLA_HEADS = 4
GLA_DV = GLA_WIDTH // GLA_HEADS
GLA_DK = GLA_DV // 2
GLA_KW = GLA_HEADS * GLA_DK
GATE_RANK = 16
GATE_TAU = 16.0
POOL_WINDOWS = (2, 4, 8, 16)
POOL_GROUPS = len(POOL_WINDOWS)
POOL_GC = POOL_WIDTH // POOL_GROUPS
D_FF = 4 * D_MODEL
EPS = 1e-6
SPLIT_POINTS = (
    GLA_KW,
    2 * GLA_KW,
    2 * GLA_KW + GLA_WIDTH,
    2 * GLA_KW + 2 * GLA_WIDTH,
    2 * GLA_KW + 2 * GLA_WIDTH + GATE_RANK,
)
D_IN = 2 * GLA_KW + 2 * GLA_WIDTH + GATE_RANK + POOL_WIDTH

kernel_name = "hybrid_gla_multiscale_pool_meta"


def rmsnorm(x, w):
    xf = x.astype(jnp.float32)
    y = xf * lax.rsqrt(jnp.mean(xf * xf, axis=-1, keepdims=True) + EPS)
    return (y * w.astype(jnp.float32)).astype(x.dtype)


def gla_chunked(q, k, v, logg):
    B, T, H, DK = q.shape
    DV = v.shape[-1]
    N = T // CHUNK

    def chunks(a):
        return a.reshape(B, N, CHUNK, H, a.shape[-1]).transpose(0, 3, 1, 2, 4)

    q, k, v, logg = chunks(q), chunks(k), chunks(v), chunks(logg)
    G = jnp.cumsum(logg, axis=3)
    G_last = G[:, :, :, -1:, :]
    q_dec = q * jnp.exp(G)
    k_inv = k * jnp.exp(-G)
    causal = jnp.tril(jnp.ones((CHUNK, CHUNK), dtype=bool))
    scores = jnp.einsum('bhncd,bhnsd->bhncs', q_dec, k_inv)
    scores = jnp.where(causal, scores, 0.0)
    o_intra = jnp.einsum('bhncs,bhnsv->bhncv', scores, v)
    k_to_end = k * jnp.exp(G_last - G)
    dS = jnp.einsum('bhncd,bhncv->bhndv', k_to_end, v)
    decay = jnp.exp(G_last[:, :, :, 0, :])

    def step(S, inp):
        dec, ds = inp
        return dec[..., None] * S + ds, S

    S0 = jnp.zeros((B, H, DK, DV), jnp.float32)
    _, S_prev = lax.scan(step, S0, (decay.transpose(2, 0, 1, 3), dS.transpose(2, 0, 1, 3, 4)))
    S_prev = S_prev.transpose(1, 2, 0, 3, 4)
    o_inter = jnp.einsum('bhncd,bhndv->bhncv', q_dec, S_prev)
    o = o_intra + o_inter
    return o.transpose(0, 2, 3, 1, 4).reshape(B, T, H, DV)


def multiscale_pool(pu, pool_w, pool_scale):
    B, L, _ = pu.shape
    xg = pu.astype(jnp.float32).reshape(B, L, POOL_GROUPS, POOL_GC)
    cs = jnp.pad(jnp.cumsum(xg, axis=1), ((0, 0), (1, 0), (0, 0), (0, 0)))
    t = jnp.arange(L)
    win = jnp.array(POOL_WINDOWS, dtype=jnp.int32)
    lo = jnp.maximum(t[:, None] + 1 - win[None, :], 0)
    g_idx = jnp.arange(POOL_GROUPS)[None, :]
    window_sum = cs[:, 1:] - cs[:, lo, g_idx]
    count = (t[:, None] + 1 - lo).astype(jnp.float32)[None, :, :, None]
    y = window_sum / count - xg
    y = jnp.einsum('blgc,gcd->blgd', y, pool_w.astype(jnp.float32))
    return y.reshape(B, L, POOL_WIDTH) * pool_scale.astype(jnp.float32)


def hybrid_layer(h, norm1_w, w_in, gate_w2, gate_b, gla_norm_w, pool_w, pool_scale,
                 w_out, norm2_w, mlp_w1, mlp_w2):
    B, L, _ = h.shape
    u = rmsnorm(h, norm1_w)
    proj = u @ w_in
    q, k, v, r, glr, pu = jnp.split(proj, SPLIT_POINTS, axis=-1)

    g_raw = (glr @ gate_w2 + gate_b).astype(jnp.float32)
    logg = jax.nn.log_sigmoid(g_raw) / GATE_TAU
    pad = (-N_META) % CHUNK

    def heads(a, d):
        a = a.astype(jnp.float32).reshape(B, L, GLA_HEADS, d)
        return jnp.pad(a, ((0, 0), (pad, 0), (0, 0), (0, 0)))

    o = gla_chunked(heads(q, GLA_DK) * (GLA_DK ** -0.5), heads(k, GLA_DK),
                    heads(v, GLA_DV), heads(logg, GLA_DK))[:, pad:]
    o = rmsnorm(o, gla_norm_w)
    gate_out = jax.nn.silu(r.astype(jnp.float32)).reshape(B, L, GLA_HEADS, GLA_DV)
    o_gla = (o * gate_out).reshape(B, L, GLA_WIDTH)

    o_pool = multiscale_pool(pu, pool_w, pool_scale)

    mixed = jnp.concatenate([o_gla, o_pool], axis=-1).astype(h.dtype)
    h = h + mixed @ w_out

    z = rmsnorm(h, norm2_w) @ mlp_w1
    h = h + jnp.square(jax.nn.relu(z)) @ mlp_w2
    return h


def setup_inputs(seed: int = 0) -> dict:
    key = jax.random.key(seed)
    ks = jax.random.split(key, 16)
    f32 = jnp.float32
    nrm = lambda k, shape, s: jax.random.normal(k, shape, f32) * s
    return {
        "x": nrm(ks[0], (BATCH, SEQ, D_MODEL), 1.0),
        "meta_tokens": nrm(ks[1], (N_META, D_MODEL), 1.0),
        "norm1_w": 1.0 + nrm(ks[2], (DEPTH, D_MODEL), 0.02),
        "w_in": nrm(ks[3], (DEPTH, D_MODEL, D_IN), D_MODEL ** -0.5),
        "gate_w2": nrm(ks[4], (DEPTH, GATE_RANK, GLA_KW), GATE_RANK ** -0.5),
        "gate_b": nrm(ks[5], (DEPTH, GLA_KW), 0.1),
        "gla_norm_w": 1.0 + nrm(ks[6], (DEPTH, GLA_DV), 0.02),
        "pool_w": nrm(ks[7], (DEPTH, POOL_GROUPS, POOL_GC, POOL_GC), POOL_GC ** -0.5),
        "pool_scale": 1.0 + nrm(ks[8], (DEPTH, POOL_WIDTH), 0.1),
        "w_out": nrm(ks[9], (DEPTH, MIX_WIDTH, D_MODEL), MIX_WIDTH ** -0.5),
        "norm2_w": 1.0 + nrm(ks[10], (DEPTH, D_MODEL), 0.02),
        "mlp_w1": nrm(ks[11], (DEPTH, D_MODEL, D_FF), D_MODEL ** -0.5),
        "mlp_w2": nrm(ks[12], (DEPTH, D_FF, D_MODEL), D_FF ** -0.5),
        "final_norm_w": 1.0 + nrm(ks[13], (D_MODEL,), 0.02),
    }


def reference(x, meta_tokens, norm1_w, w_in, gate_w2, gate_b, gla_norm_w, pool_w,
              pool_scale, w_out, norm2_w, mlp_w1, mlp_w2, final_norm_w):
    B = x.shape[0]
    meta = jnp.broadcast_to(meta_tokens[None].astype(x.dtype), (B, N_META, D_MODEL))
    h = jnp.concatenate([meta, x], axis=1)
    for i in range(DEPTH):
        h = hybrid_layer(h, norm1_w[i], w_in[i], gate_w2[i], gate_b[i], gla_norm_w[i],
                         pool_w[i], pool_scale[i], w_out[i], norm2_w[i], mlp_w1[i], mlp_w2[i])
    h = rmsnorm(h, final_norm_w)
    return h[:, N_META:]
```

```python
import functools

import jax
import jax.numpy as jnp
from jax import lax
from jax.experimental import pallas as pl
from jax.experimental.pallas import tpu as pltpu

F32 = jnp.float32
BF16 = jnp.bfloat16

N_META = 16
CHUNK = 64
PAIR = 2 * CHUNK
GLA_HEADS = 4
GLA_DK = 128
GLA_DV = 256
GLA_KW = GLA_HEADS * GLA_DK
GLA_WIDTH = GLA_HEADS * GLA_DV
GATE_RANK = 16
GATE_TAU = 16.0
POOL_WINDOWS = (2, 4, 8, 16)
POOL_GC = 256
POOL_WIDTH = POOL_GC * len(POOL_WINDOWS)
POOL_HIST = 16
EPS = 1e-6
LANES = 128
VMEM_LIMIT = 56 * 1024 * 1024

PROJ_TM, PROJ_TN = 1024, 1024
MIX_TB = 256
WOUT_TM, WOUT_TN = 1024, 1024
MLP_TM, MLP_TF = 1024, 512


def _rms_scale(x):
    return lax.rsqrt(jnp.mean(x * x, axis=-1, keepdims=True) + EPS)


def _dot(a, b):
    return jnp.dot(a, b, preferred_element_type=F32)


def _dot_nt(a, b):
    return lax.dot_general(a, b, (((1,), (1,)), ((), ())), preferred_element_type=F32)


def _proj_kernel(x_ref, nw_ref, wm_ref, wg_ref, o_ref, g_ref, xn_ref):
    @pl.when(pl.program_id(1) == 0)
    def _():
        x = x_ref[...]
        xn = (x * _rms_scale(x) * nw_ref[...]).astype(BF16)
        xn_ref[...] = xn
        g_ref[...] = _dot(xn, wg_ref[...])

    o_ref[...] = _dot(xn_ref[...], wm_ref[...]).astype(o_ref.dtype)


def _proj(x2d, nw, wm, wg, tm, tn):
    m, d = x2d.shape
    n = wm.shape[1]
    return pl.pallas_call(
        _proj_kernel,
        grid=(m // tm, n // tn),
        in_specs=[
            pl.BlockSpec((tm, d), lambda i, j: (i, 0)),
            pl.BlockSpec((1, d), lambda i, j: (0, 0)),
            pl.BlockSpec((d, tn), lambda i, j: (0, j)),
            pl.BlockSpec((d, LANES), lambda i, j: (0, 0)),
        ],
        out_specs=[
            pl.BlockSpec((tm, tn), lambda i, j: (i, j)),
            pl.BlockSpec((tm, LANES), lambda i, j: (i, 0)),
        ],
        out_shape=[
            jax.ShapeDtypeStruct((m, n), BF16),
            jax.ShapeDtypeStruct((m, LANES), F32),
        ],
        scratch_shapes=[pltpu.VMEM((tm, d), BF16)],
        compiler_params=pltpu.CompilerParams(
            dimension_semantics=("parallel", "arbitrary"),
            vmem_limit_bytes=VMEM_LIMIT),
        name="proj",
    )(x2d, nw, wm, wg)


def _log_gate(glr, gw2_ref, gb_ref):
    g = _dot(glr.astype(BF16), gw2_ref[...]) + gb_ref[...]
    return (jnp.minimum(g, 0.0) - jnp.log(1.0 + jnp.exp(-jnp.abs(g)))) * (1.0 / GATE_TAU)


def _rows_matmul_f32(sel_bf16, x):
    hi = x.astype(BF16)
    lo = (x - hi.astype(F32)).astype(BF16)
    return _dot(sel_bf16, hi) + _dot(sel_bf16, lo)


def _mixer_kernel(q_ref, k_ref, v_ref, r_ref, pu_ref, glr_ref, mp_ref, mg_ref,
                  gw2_ref, gb_ref, gnw_ref, pw_ref, ps_ref,
                  o_ref, st_ref, ext_ref):
    tb = q_ref.shape[0]

    @pl.when(pl.program_id(1) == 0)
    def _():
        npad = PAIR - N_META
        logg = jnp.concatenate(
            [jnp.zeros((npad, GLA_KW), F32), _log_gate(mg_ref[...], gw2_ref, gb_ref)], axis=0)
        row = lax.broadcasted_iota(jnp.int32, (PAIR, PAIR), 0)
        col = lax.broadcasted_iota(jnp.int32, (PAIR, PAIR), 1)
        suffix = jnp.where(col > row, 1.0, 0.0).astype(BF16)
        to_end = _rows_matmul_f32(suffix, logg)
        k_m = mp_ref[:, GLA_KW:2 * GLA_KW].astype(F32)
        k_end = (jnp.concatenate([jnp.zeros((npad, GLA_KW), F32), k_m], axis=0)
                 * jnp.exp(to_end)).astype(BF16)
        v_m = mp_ref[:, 2 * GLA_KW:2 * GLA_KW + GLA_WIDTH].astype(F32)
        v_t = jnp.concatenate([jnp.zeros((npad, GLA_WIDTH), F32), v_m], axis=0).T.astype(BF16)
        for h in range(GLA_HEADS):
            st_ref[h] = _dot(v_t[h * GLA_DV:(h + 1) * GLA_DV, :],
                             k_end[:, h * GLA_DK:(h + 1) * GLA_DK])
        ext_ref[0:POOL_HIST, :] = mp_ref[:, 2 * GLA_KW + 2 * GLA_WIDTH:].astype(F32)

    logg = _log_gate(glr_ref[...], gw2_ref, gb_ref)
    row = lax.broadcasted_iota(jnp.int32, (tb, tb), 0)
    col = lax.broadcasted_iota(jnp.int32, (tb, tb), 1)
    same_chunk = (row // CHUNK) == (col // CHUNK)
    cum_sel = jnp.where(same_chunk & (col <= row), 1.0, 0.0).astype(BF16)
    tot_sel = jnp.where(same_chunk, 1.0, 0.0).astype(BF16)
    g_cum = _rows_matmul_f32(cum_sel, logg)
    g_tot = _rows_matmul_f32(tot_sel, logg)
    causal = (same_chunk & (col <= row))[0:PAIR, 0:PAIR]
    first = lax.broadcasted_iota(jnp.int32, (PAIR, 1), 0) < CHUNK

    v_t = v_ref[...].astype(F32).T.astype(BF16)
    gnw = gnw_ref[...]
    for p in range(tb // PAIR):
        rs = slice(p * PAIR, (p + 1) * PAIR)
        gc = g_cum[rs]
        gt = g_tot[rs]
        qf = q_ref[rs, :].astype(F32)
        kf = k_ref[rs, :].astype(F32)
        q_dec = (qf * (GLA_DK ** -0.5) * jnp.exp(gc)).astype(BF16)
        k_inv = (kf * jnp.exp(-gc)).astype(BF16)
        k_end = kf * jnp.exp(gt - gc)
        k_end0 = jnp.where(first, k_end, 0.0).astype(BF16)
        k_end1 = jnp.where(first, 0.0, k_end).astype(BF16)
        dec0 = jnp.exp(gt[0:1])
        dec1 = jnp.exp(gt[CHUNK:CHUNK + 1])
        for h in range(GLA_HEADS):
            ck = slice(h * GLA_DK, (h + 1) * GLA_DK)
            cv = slice(h * GLA_DV, (h + 1) * GLA_DV)
            scores = jnp.where(causal, _dot_nt(q_dec[:, ck], k_inv[:, ck]), 0.0).astype(BF16)
            o = _dot(scores, v_ref[rs, cv])
            st0 = st_ref[h]
            v_th = v_t[cv, rs]
            st1 = st0 * dec0[:, ck] + _dot(v_th, k_end0[:, ck])
            st_ref[h] = st1 * dec1[:, ck] + _dot(v_th, k_end1[:, ck])
            o = o + jnp.concatenate(
                [_dot_nt(q_dec[0:CHUNK, ck], st0.astype(BF16)),
                 _dot_nt(q_dec[CHUNK:PAIR, ck], st1.astype(BF16))], axis=0)
            o = o * _rms_scale(o) * gnw
            rf = r_ref[rs, cv].astype(F32)
            gate = rf * (1.0 / (1.0 + jnp.exp(-rf)))
            o_ref[rs, cv] = (o * gate).astype(o_ref.dtype)

    ext_ref[POOL_HIST:POOL_HIST + tb, :] = pu_ref[...].astype(F32)
    for g, w in enumerate(POOL_WINDOWS):
        cs = slice(g * POOL_GC, (g + 1) * POOL_GC)
        cur = ext_ref[POOL_HIST:POOL_HIST + tb, cs]
        acc = cur
        for j in range(1, w):
            acc = acc + ext_ref[POOL_HIST - j:POOL_HIST - j + tb, cs]
        y = (acc * (1.0 / w) - cur).astype(BF16)
        y = _dot(y, pw_ref[g]) * ps_ref[:, cs]
        o_ref[:, GLA_WIDTH + g * POOL_GC:GLA_WIDTH + (g + 1) * POOL_GC] = y.astype(o_ref.dtype)
    ext_ref[0:POOL_HIST, :] = ext_ref[tb:tb + POOL_HIST, :]


def _mixer(proj, glr, meta_proj, meta_glr, gw2, gb, gnw, pw, ps, batch, seq, tb):
    nt = seq // tb
    tok = lambda c: (lambda b, i: (b * nt + i, c))
    full = lambda shape: pl.BlockSpec(shape, lambda b, i: (0,) * len(shape))
    return pl.pallas_call(
        _mixer_kernel,
        grid=(batch, nt),
        in_specs=[
            pl.BlockSpec((tb, GLA_KW), tok(0)),
            pl.BlockSpec((tb, GLA_KW), tok(1)),
            pl.BlockSpec((tb, GLA_WIDTH), tok(1)),
            pl.BlockSpec((tb, GLA_WIDTH), tok(2)),
            pl.BlockSpec((tb, POOL_WIDTH), tok(3)),
            pl.BlockSpec((tb, LANES), tok(0)),
            full(meta_proj.shape),
            full(meta_glr.shape),
            full(gw2.shape),
            full(gb.shape),
            full(gnw.shape),
            full(pw.shape),
            full(ps.shape),
        ],
        out_specs=pl.BlockSpec((tb, GLA_WIDTH + POOL_WIDTH), tok(0)),
        out_shape=jax.ShapeDtypeStruct((batch * seq, GLA_WIDTH + POOL_WIDTH), BF16),
        scratch_shapes=[
            pltpu.VMEM((GLA_HEADS, GLA_DV, GLA_DK), F32),
            pltpu.VMEM((tb + POOL_HIST, POOL_WIDTH), F32),
        ],
        compiler_params=pltpu.CompilerParams(
            dimension_semantics=("parallel", "arbitrary"),
            vmem_limit_bytes=VMEM_LIMIT),
        name="mixer",
    )(proj, proj, proj, proj, proj, glr, meta_proj, meta_glr, gw2, gb, gnw, pw, ps)


def _wout_kernel(m_ref, w_ref, x_ref, o_ref):
    o_ref[...] = x_ref[...] + _dot(m_ref[...], w_ref[...])


def _wout(mixed, w, x2d, tm, tn):
    m, kdim = mixed.shape
    n = w.shape[1]
    return pl.pallas_call(
        _wout_kernel,
        grid=(m // tm, n // tn),
        in_specs=[
            pl.BlockSpec((tm, kdim), lambda i, j: (i, 0)),
            pl.BlockSpec((kdim, tn), lambda i, j: (0, j)),
            pl.BlockSpec((tm, tn), lambda i, j: (i, j)),
        ],
        out_specs=pl.BlockSpec((tm, tn), lambda i, j: (i, j)),
        out_shape=jax.ShapeDtypeStruct((m, n), F32),
        compiler_params=pltpu.CompilerParams(
            dimension_semantics=("parallel", "arbitrary"),
            vmem_limit_bytes=VMEM_LIMIT),
        name="wout",
    )(mixed, w, x2d)


def _mlp_kernel(h_ref, nw_ref, w1_ref, w2_ref, fnw_ref, o_ref, xn_ref):
    f = pl.program_id(1)

    @pl.when(f == 0)
    def _():
        h = h_ref[...]
        xn_ref[...] = (h * _rms_scale(h) * nw_ref[...]).astype(BF16)
        o_ref[...] = h

    z = _dot(xn_ref[...], w1_ref[...])
    a = jnp.square(jnp.maximum(z, 0.0)).astype(BF16)
    o_ref[...] += _dot(a, w2_ref[...])

    @pl.when(f == pl.num_programs(1) - 1)
    def _():
        y = o_ref[...]
        o_ref[...] = y * _rms_scale(y) * fnw_ref[...]


def _mlp(h, nw, w1, w2, fnw, tm, tf):
    m, d = h.shape
    ff = w1.shape[1]
    return pl.pallas_call(
        _mlp_kernel,
        grid=(m // tm, ff // tf),
        in_specs=[
            pl.BlockSpec((tm, d), lambda i, j: (i, 0)),
            pl.BlockSpec((1, d), lambda i, j: (0, 0)),
            pl.BlockSpec((d, tf), lambda i, j: (0, j)),
            pl.BlockSpec((tf, d), lambda i, j: (j, 0)),
            pl.BlockSpec((1, d), lambda i, j: (0, 0)),
        ],
        out_specs=pl.BlockSpec((tm, d), lambda i, j: (i, 0)),
        out_shape=jax.ShapeDtypeStruct((m, d), F32),
        scratch_shapes=[pltpu.VMEM((tm, d), BF16)],
        compiler_params=pltpu.CompilerParams(
            dimension_semantics=("parallel", "arbitrary"),
            vmem_limit_bytes=VMEM_LIMIT),
        name="mlp",
    )(h, nw, w1, w2, fnw)


def kernel(x, meta_tokens, norm1_w, w_in, gate_w2, gate_b, gla_norm_w, pool_w, pool_scale,
           w_out, norm2_w, mlp_w1, mlp_w2, final_norm_w):
    batch, seq, d = x.shape
    assert w_in.shape[0] == 1, "single-layer trunk"
    assert meta_tokens.shape[0] == N_META

    glr_lo = 2 * GLA_KW + 2 * GLA_WIDTH
    glr_hi = glr_lo + GATE_RANK
    w_in0 = w_in[0]
    wm = jnp.concatenate([w_in0[:, :glr_lo], w_in0[:, glr_hi:]], axis=1).astype(BF16)
    wg = jnp.pad(w_in0[:, glr_lo:glr_hi], ((0, 0), (0, LANES - GATE_RANK))).astype(BF16)
    gw2 = jnp.pad(gate_w2[0], ((0, LANES - GATE_RANK), (0, 0))).astype(BF16)
    gb = gate_b[0].reshape(1, GLA_KW)
    gnw = gla_norm_w[0].reshape(1, GLA_DV)
    pw = pool_w[0].astype(BF16)
    ps = pool_scale[0].reshape(1, POOL_WIDTH)
    nw1 = norm1_w[0].reshape(1, d)
    nw2 = norm2_w[0].reshape(1, d)
    fnw = final_norm_w.reshape(1, d)

    x2d = x.reshape(batch * seq, d)
    proj, glr = _proj(x2d, nw1, wm, wg, PROJ_TM, PROJ_TN)
    meta_proj, meta_glr = _proj(meta_tokens.astype(x.dtype), nw1, wm, wg, N_META, PROJ_TN)
    mixed = _mixer(proj, glr, meta_proj, meta_glr, gw2, gb, gnw, pw, ps, batch, seq, MIX_TB)
    h1 = _wout(mixed, w_out[0].astype(BF16), x2d, WOUT_TM, WOUT_TN)
    out = _mlp(h1, nw2, mlp_w1[0].astype(BF16), mlp_w2[0].astype(BF16), fnw, MLP_TM, MLP_TF)
    return out.reshape(batch, seq, d)
```

```python
import functools

import jax
import jax.numpy as jnp
from jax import lax
from jax.experimental import pallas as pl
from jax.experimental.pallas import tpu as pltpu

F32 = jnp.float32
BF16 = jnp.bfloat16

N_META = 16
CHUNK = 64
PAIR = 2 * CHUNK
GLA_HEADS = 4
GLA_DK = 128
GLA_DV = 256
GLA_KW = GLA_HEADS * GLA_DK
GLA_WIDTH = GLA_HEADS * GLA_DV
GATE_RANK = 16
GATE_TAU = 16.0
POOL_WINDOWS = (2, 4, 8, 16)
POOL_GC = 256
POOL_WIDTH = POOL_GC * len(POOL_WINDOWS)
POOL_HIST = 16
EPS = 1e-6
LANES = 128
VMEM_LIMIT = 56 * 1024 * 1024

PROJ_TM, PROJ_TN = 1024, 1024
MIX_TB = 256
WOUT_TM, WOUT_TN = 1024, 1024
MLP_TM, MLP_TF = 1024, 512


def _rms_scale(x):
    return lax.rsqrt(jnp.mean(x * x, axis=-1, keepdims=True) + EPS)


def _dot(a, b):
    return jnp.dot(a, b, preferred_element_type=F32)


def _dot_nt(a, b):
    return lax.dot_general(a, b, (((1,), (1,)), ((), ())), preferred_element_type=F32)


def _proj_kernel(x_ref, nw_ref, wa_ref, wb_ref, wg_ref, o_ref, g_ref, xn_ref):
    n = pl.program_id(1)
    last = pl.num_programs(1) - 1

    @pl.when(n == 0)
    def _():
        x = x_ref[...]
        xn = (x * _rms_scale(x) * nw_ref[...]).astype(BF16)
        xn_ref[...] = xn
        g_ref[...] = _dot(xn, wg_ref[...])

    @pl.when(n < last)
    def _():
        o_ref[...] = _dot(xn_ref[...], wa_ref[...]).astype(o_ref.dtype)

    @pl.when(n == last)
    def _():
        o_ref[...] = _dot(xn_ref[...], wb_ref[...]).astype(o_ref.dtype)


def _proj(x2d, nw, wa, wb, wg, tm, tn):
    m, d = x2d.shape
    assert wb.shape[1] == tn
    na = wa.shape[1] // tn
    n = wa.shape[1] + tn
    return pl.pallas_call(
        _proj_kernel,
        grid=(m // tm, na + 1),
        in_specs=[
            pl.BlockSpec((tm, d), lambda i, j: (i, 0)),
            pl.BlockSpec((1, d), lambda i, j: (0, 0)),
            pl.BlockSpec((d, tn), lambda i, j: (0, jnp.minimum(j, na - 1))),
            pl.BlockSpec((d, tn), lambda i, j: (0, 0)),
            pl.BlockSpec((d, LANES), lambda i, j: (0, 0)),
        ],
        out_specs=[
            pl.BlockSpec((tm, tn), lambda i, j: (i, j)),
            pl.BlockSpec((tm, LANES), lambda i, j: (i, 0)),
        ],
        out_shape=[
            jax.ShapeDtypeStruct((m, n), BF16),
            jax.ShapeDtypeStruct((m, LANES), F32),
        ],
        scratch_shapes=[pltpu.VMEM((tm, d), BF16)],
        compiler_params=pltpu.CompilerParams(
            dimension_semantics=("parallel", "arbitrary"),
            vmem_limit_bytes=VMEM_LIMIT),
        name="proj",
    )(x2d, nw, wa, wb, wg)


def _log_gate(glr, gw2_ref, gb_ref):
    g = _dot(glr.astype(BF16), gw2_ref[...]) + gb_ref[...]
    return (jnp.minimum(g, 0.0) - jnp.log(1.0 + jnp.exp(-jnp.abs(g)))) * (1.0 / GATE_TAU)


def _rows_matmul_f32(sel_bf16, x):
    hi = x.astype(BF16)
    lo = (x - hi.astype(F32)).astype(BF16)
    return _dot(sel_bf16, hi) + _dot(sel_bf16, lo)


def _mixer_kernel(q_ref, k_ref, v_ref, r_ref, pu_ref, glr_ref, mp_ref, mg_ref,
                  gw2_ref, gb_ref, gnw_ref, pw_ref, ps_ref,
                  o_ref, st_ref, ext_ref):
    tb = q_ref.shape[0]

    @pl.when(pl.program_id(1) == 0)
    def _():
        npad = PAIR - N_META
        logg = jnp.concatenate(
            [jnp.zeros((npad, GLA_KW), F32), _log_gate(mg_ref[...], gw2_ref, gb_ref)], axis=0)
        row = lax.broadcasted_iota(jnp.int32, (PAIR, PAIR), 0)
        col = lax.broadcasted_iota(jnp.int32, (PAIR, PAIR), 1)
        suffix = jnp.where(col > row, 1.0, 0.0).astype(BF16)
        to_end = _rows_matmul_f32(suffix, logg)
        k_m = mp_ref[:, GLA_KW:2 * GLA_KW].astype(F32)
        k_end = (jnp.concatenate([jnp.zeros((npad, GLA_KW), F32), k_m], axis=0)
                 * jnp.exp(to_end)).astype(BF16)
        v_m = mp_ref[:, 2 * GLA_KW:2 * GLA_KW + GLA_WIDTH].astype(F32)
        v_t = jnp.concatenate([jnp.zeros((npad, GLA_WIDTH), F32), v_m], axis=0).T.astype(BF16)
        for h in range(GLA_HEADS):
            st_ref[h] = _dot(v_t[h * GLA_DV:(h + 1) * GLA_DV, :],
                             k_end[:, h * GLA_DK:(h + 1) * GLA_DK])
        ext_ref[0:POOL_HIST, :] = mp_ref[:, 2 * GLA_KW + 2 * GLA_WIDTH:].astype(F32)

    logg = _log_gate(glr_ref[...], gw2_ref, gb_ref)
    row = lax.broadcasted_iota(jnp.int32, (tb, tb), 0)
    col = lax.broadcasted_iota(jnp.int32, (tb, tb), 1)
    same_chunk = (row // CHUNK) == (col // CHUNK)
    cum_sel = jnp.where(same_chunk & (col <= row), 1.0, 0.0).astype(BF16)
    tot_sel = jnp.where(same_chunk, 1.0, 0.0).astype(BF16)
    g_cum = _rows_matmul_f32(cum_sel, logg)
    g_tot = _rows_matmul_f32(tot_sel, logg)
    causal = (same_chunk & (col <= row))[0:PAIR, 0:PAIR]
    first = lax.broadcasted_iota(jnp.int32, (PAIR, 1), 0) < CHUNK

    v_t = v_ref[...].astype(F32).T.astype(BF16)
    gnw = gnw_ref[...]
    for p in range(tb // PAIR):
        rs = slice(p * PAIR, (p + 1) * PAIR)
        gc = g_cum[rs]
        gt = g_tot[rs]
        qf = q_ref[rs, :].astype(F32)
        kf = k_ref[rs, :].astype(F32)
        q_dec = (qf * (GLA_DK ** -0.5) * jnp.exp(gc)).astype(BF16)
        k_inv = (kf * jnp.exp(-gc)).astype(BF16)
        k_end = kf * jnp.exp(gt - gc)
        k_end0 = jnp.where(first, k_end, 0.0).astype(BF16)
        k_end1 = jnp.where(first, 0.0, k_end).astype(BF16)
        dec0 = jnp.exp(gt[0:1])
        dec1 = jnp.exp(gt[CHUNK:CHUNK + 1])
        for h in range(GLA_HEADS):
            ck = slice(h * GLA_DK, (h + 1) * GLA_DK)
            cv = slice(h * GLA_DV, (h + 1) * GLA_DV)
            scores = jnp.where(causal, _dot_nt(q_dec[:, ck], k_inv[:, ck]), 0.0).astype(BF16)
            o = _dot(scores, v_ref[rs, cv])
            st0 = st_ref[h]
            v_th = v_t[cv, rs]
            st1 = st0 * dec0[:, ck] + _dot(v_th, k_end0[:, ck])
            st_ref[h] = st1 * dec1[:, ck] + _dot(v_th, k_end1[:, ck])
            o = o + jnp.concatenate(
                [_dot_nt(q_dec[0:CHUNK, ck], st0.astype(BF16)),
                 _dot_nt(q_dec[CHUNK:PAIR, ck], st1.astype(BF16))], axis=0)
            o = o * _rms_scale(o) * gnw
            rf = r_ref[rs, cv].astype(F32)
            gate = rf * (1.0 / (1.0 + jnp.exp(-rf)))
            o_ref[rs, cv] = (o * gate).astype(o_ref.dtype)

    ext_ref[POOL_HIST:POOL_HIST + tb, :] = pu_ref[...].astype(F32)
    for g, w in enumerate(POOL_WINDOWS):
        cs = slice(g * POOL_GC, (g + 1) * POOL_GC)
        cur = ext_ref[POOL_HIST:POOL_HIST + tb, cs]
        acc = cur
        for j in range(1, w):
            acc = acc + ext_ref[POOL_HIST - j:POOL_HIST - j + tb, cs]
        y = (acc * (1.0 / w) - cur).astype(BF16)
        y = _dot(y, pw_ref[g]) * ps_ref[:, cs]
        o_ref[:, GLA_WIDTH + g * POOL_GC:GLA_WIDTH + (g + 1) * POOL_GC] = y.astype(o_ref.dtype)
    ext_ref[0:POOL_HIST, :] = ext_ref[tb:tb + POOL_HIST, :]


def _mixer(proj, glr, meta_proj, meta_glr, gw2, gb, gnw, pw, ps, batch, seq, tb):
    nt = seq // tb
    tok = lambda c: (lambda b, i: (b * nt + i, c))
    full = lambda shape: pl.BlockSpec(shape, lambda b, i: (0,) * len(shape))
    return pl.pallas_call(
        _mixer_kernel,
        grid=(batch, nt),
        in_specs=[
            pl.BlockSpec((tb, GLA_KW), tok(0)),
            pl.BlockSpec((tb, GLA_KW), tok(1)),
            pl.BlockSpec((tb, GLA_WIDTH), tok(1)),
            pl.BlockSpec((tb, GLA_WIDTH), tok(2)),
            pl.BlockSpec((tb, POOL_WIDTH), tok(3)),
            pl.BlockSpec((tb, LANES), tok(0)),
            full(meta_proj.shape),
            full(meta_glr.shape),
            full(gw2.shape),
            full(gb.shape),
            full(gnw.shape),
            full(pw.shape),
            full(ps.shape),
        ],
        out_specs=pl.BlockSpec((tb, GLA_WIDTH + POOL_WIDTH), tok(0)),
        out_shape=jax.ShapeDtypeStruct((batch * seq, GLA_WIDTH + POOL_WIDTH), BF16),
        scratch_shapes=[
            pltpu.VMEM((GLA_HEADS, GLA_DV, GLA_DK), F32),
            pltpu.VMEM((tb + POOL_HIST, POOL_WIDTH), F32),
        ],
        compiler_params=pltpu.CompilerParams(
            dimension_semantics=("parallel", "arbitrary"),
            vmem_limit_bytes=VMEM_LIMIT),
        name="mixer",
    )(proj, proj, proj, proj, proj, glr, meta_proj, meta_glr, gw2, gb, gnw, pw, ps)


def _wout_kernel(m_ref, w_ref, x_ref, o_ref):
    o_ref[...] = x_ref[...] + _dot(m_ref[...], w_ref[...].astype(BF16))


def _wout(mixed, w, x2d, tm, tn):
    m, kdim = mixed.shape
    n = w.shape[1]
    return pl.pallas_call(
        _wout_kernel,
        grid=(m // tm, n // tn),
        in_specs=[
            pl.BlockSpec((tm, kdim), lambda i, j: (i, 0)),
            pl.BlockSpec((kdim, tn), lambda i, j: (0, j)),
            pl.BlockSpec((tm, tn), lambda i, j: (i, j)),
        ],
        out_specs=pl.BlockSpec((tm, tn), lambda i, j: (i, j)),
        out_shape=jax.ShapeDtypeStruct((m, n), F32),
        compiler_params=pltpu.CompilerParams(
            dimension_semantics=("parallel", "arbitrary"),
            vmem_limit_bytes=VMEM_LIMIT),
        name="wout",
    )(mixed, w, x2d)


def _mlp_kernel(h_ref, nw_ref, w1_ref, w2_ref, fnw_ref, o_ref, xn_ref):
    f = pl.program_id(1)

    @pl.when(f == 0)
    def _():
        h = h_ref[...]
        xn_ref[...] = (h * _rms_scale(h) * nw_ref[...]).astype(BF16)
        o_ref[...] = h

    z = _dot(xn_ref[...], w1_ref[...].astype(BF16))
    a = jnp.square(jnp.maximum(z, 0.0)).astype(BF16)
    o_ref[...] += _dot(a, w2_ref[...].astype(BF16))

    @pl.when(f == pl.num_programs(1) - 1)
    def _():
        y = o_ref[...]
        o_ref[...] = y * _rms_scale(y) * fnw_ref[...]


def _mlp(h, nw, w1, w2, fnw, tm, tf):
    m, d = h.shape
    ff = w1.shape[1]
    return pl.pallas_call(
        _mlp_kernel,
        grid=(m // tm, ff // tf),
        in_specs=[
            pl.BlockSpec((tm, d), lambda i, j: (i, 0), pipeline_mode=pl.Buffered(1)),
            pl.BlockSpec((1, d), lambda i, j: (0, 0)),
            pl.BlockSpec((d, tf), lambda i, j: (0, j)),
            pl.BlockSpec((tf, d), lambda i, j: (j, 0)),
            pl.BlockSpec((1, d), lambda i, j: (0, 0)),
        ],
        out_specs=pl.BlockSpec((tm, d), lambda i, j: (i, 0)),
        out_shape=jax.ShapeDtypeStruct((m, d), F32),
        scratch_shapes=[pltpu.VMEM((tm, d), BF16)],
        compiler_params=pltpu.CompilerParams(
            dimension_semantics=("parallel", "arbitrary"),
            vmem_limit_bytes=VMEM_LIMIT),
        name="mlp",
    )(h, nw, w1, w2, fnw)


def kernel(x, meta_tokens, norm1_w, w_in, gate_w2, gate_b, gla_norm_w, pool_w, pool_scale,
           w_out, norm2_w, mlp_w1, mlp_w2, final_norm_w):
    batch, seq, d = x.shape
    assert w_in.shape[0] == 1, "single-layer trunk"
    assert meta_tokens.shape[0] == N_META

    glr_lo = 2 * GLA_KW + 2 * GLA_WIDTH
    glr_hi = glr_lo + GATE_RANK
    w_in0 = w_in[0]
    wa = w_in0[:, :glr_lo].astype(BF16)
    wb = w_in0[:, glr_hi:].astype(BF16)
    wg =jnp.pad(w_in0[:, glr_lo:glr_hi], ((0, 0), (0, LANES - GATE_RANK))).astype(BF16)
    gw2 = jnp.pad(gate_w2[0], ((0, LANES - GATE_RANK), (0, 0))).astype(BF16)
    gb = gate_b[0].reshape(1, GLA_KW)
    gnw = gla_norm_w[0].reshape(1, GLA_DV)
    pw = pool_w[0].astype(BF16)
    ps = pool_scale[0].reshape(1, POOL_WIDTH)
    nw1 = norm1_w[0].reshape(1, d)
    nw2 = norm2_w[0].reshape(1, d)
    fnw = final_norm_w.reshape(1, d)

    x2d = x.reshape(batch * seq, d)
    proj, glr = _proj(x2d, nw1, wa, wb, wg, PROJ_TM, PROJ_TN)
    meta_proj, meta_glr = _proj(meta_tokens.astype(x.dtype), nw1, wa, wb, wg, N_META, PROJ_TN)
    mixed = _mixer(proj, glr, meta_proj, meta_glr, gw2, gb, gnw, pw, ps, batch, seq, MIX_TB)
    h1 = _wout(mixed, w_out[0], x2d, WOUT_TM, WOUT_TN)
    out = _mlp(h1, nw2, mlp_w1[0], mlp_w2[0], fnw, MLP_TM, MLP_TF)
    return out.reshape(batch, seq, d)
```

```python
import functools

import jax
import jax.numpy as jnp
from jax import lax
from jax.experimental import pallas as pl
from jax.experimental.pallas import tpu as pltpu

F32 = jnp.float32
BF16 = jnp.bfloat16

N_META = 16
CHUNK = 64
PAIR = 2 * CHUNK
GLA_HEADS = 4
GLA_DK = 128
GLA_DV = 256
GLA_KW = GLA_HEADS * GLA_DK
GLA_WIDTH = GLA_HEADS * GLA_DV
GATE_RANK = 16
GATE_TAU = 16.0
POOL_WINDOWS = (2, 4, 8, 16)
POOL_GC = 256
POOL_WIDTH = POOL_GC * len(POOL_WINDOWS)
POOL_HIST = 16
EPS = 1e-6
LANES = 128
VMEM_LIMIT = 56 * 1024 * 1024

PREP_TR = 256
PROJ_TM, PROJ_TN = 1024, 2048
MIX_TB = 256
WOUT_TM = 1024
MLP_TM, MLP_TF = 1024, 512


def _rms_scale(x):
    return lax.rsqrt(jnp.mean(x * x, axis=-1, keepdims=True) + EPS)


def _dot(a, b):
    return jnp.dot(a, b, preferred_element_type=F32)


def _dot_nt(a, b):
    return lax.dot_general(a, b, (((1,), (1,)), ((), ())), preferred_element_type=F32)


def _win_prep_kernel(w_ref, wm_ref, wg_ref):
    glr_lo = 2 * GLA_KW + 2 * GLA_WIDTH
    glr_hi = glr_lo + GATE_RANK
    w = w_ref[0]
    wm_ref[:, 0:glr_lo] = w[:, 0:glr_lo].astype(BF16)
    wm_ref[:, glr_lo:glr_lo + POOL_WIDTH] = w[:, glr_hi:glr_hi + POOL_WIDTH].astype(BF16)
    wg_ref[...] = w[:, glr_lo:glr_lo + LANES].astype(BF16)


def _win_prep(w_in, tr):
    _, d, d_in = w_in.shape
    n = d_in - GATE_RANK
    return pl.pallas_call(
        _win_prep_kernel,
        grid=(d // tr,),
        in_specs=[pl.BlockSpec((1, tr, d_in), lambda i: (0, i, 0))],
        out_specs=[
            pl.BlockSpec((tr, n), lambda i: (i, 0)),
            pl.BlockSpec((tr, LANES), lambda i: (i, 0)),
        ],
        out_shape=[
            jax.ShapeDtypeStruct((d, n), BF16),
            jax.ShapeDtypeStruct((d, LANES), BF16),
        ],
        compiler_params=pltpu.CompilerParams(
            dimension_semantics=("parallel",), vmem_limit_bytes=VMEM_LIMIT),
        name="win_prep",
    )(w_in)


def _proj_kernel(x_ref, nw_ref, wm_ref, wg_ref, o_ref, g_ref, xn_ref):
    @pl.when(pl.program_id(1) == 0)
    def _():
        x = x_ref[...]
        xn = (x * _rms_scale(x) * nw_ref[...]).astype(BF16)
        xn_ref[...] = xn
        g_ref[...] = _dot(xn, wg_ref[...])

    o_ref[...] = _dot(xn_ref[...], wm_ref[...]).astype(o_ref.dtype)


def _proj(x2d, nw, wm, wg, tm, tn):
    m, d = x2d.shape
    n = wm.shape[1]
    return pl.pallas_call(
        _proj_kernel,
        grid=(m // tm, n // tn),
        in_specs=[
            pl.BlockSpec((tm, d), lambda i, j: (i, 0)),
            pl.BlockSpec((1, d), lambda i, j: (0, 0)),
            pl.BlockSpec((d, tn), lambda i, j: (0, j)),
            pl.BlockSpec((d, LANES), lambda i, j: (0, 0)),
        ],
        out_specs=[
            pl.BlockSpec((tm, tn), lambda i, j: (i, j)),
            pl.BlockSpec((tm, LANES), lambda i, j: (i, 0)),
        ],
        out_shape=[
            jax.ShapeDtypeStruct((m, n), BF16),
            jax.ShapeDtypeStruct((m, LANES), F32),
        ],
        scratch_shapes=[pltpu.VMEM((tm, d), BF16)],
        compiler_params=pltpu.CompilerParams(
            dimension_semantics=("parallel", "arbitrary"),
            vmem_limit_bytes=VMEM_LIMIT),
        name="proj",
    )(x2d, nw, wm, wg)


def _log_gate(glr, gw2_ref, gb_ref):
    g = _dot(glr.astype(BF16), gw2_ref[...]) + gb_ref[...]
    return (jnp.minimum(g, 0.0) - jnp.log(1.0 + jnp.exp(-jnp.abs(g)))) * (1.0 / GATE_TAU)


def _rows_matmul_f32(sel_bf16, x):
    hi = x.astype(BF16)
    lo = (x - hi.astype(F32)).astype(BF16)
    return _dot(sel_bf16, hi) + _dot(sel_bf16, lo)


def _mixer_kernel(q_ref, k_ref, v_ref, r_ref, pu_ref, glr_ref, mp_ref, mg_ref,
                  gw2_ref, gb_ref, gnw_ref, pw_ref, ps_ref,
                  o_ref, st_ref, ext_ref):
    tb = q_ref.shape[0]

    @pl.when(pl.program_id(1) == 0)
    def _():
        npad = PAIR - N_META
        logg = jnp.concatenate(
            [jnp.zeros((npad, GLA_KW), F32), _log_gate(mg_ref[...], gw2_ref, gb_ref)], axis=0)
        row = lax.broadcasted_iota(jnp.int32, (PAIR, PAIR), 0)
        col = lax.broadcasted_iota(jnp.int32, (PAIR, PAIR), 1)
        suffix = jnp.where(col > row, 1.0, 0.0).astype(BF16)
        to_end = _rows_matmul_f32(suffix, logg)
        k_m = mp_ref[:, GLA_KW:2 * GLA_KW].astype(F32)
        k_end = (jnp.concatenate([jnp.zeros((npad, GLA_KW), F32), k_m], axis=0)
                 * jnp.exp(to_end)).astype(BF16)
        v_m = mp_ref[:, 2 * GLA_KW:2 * GLA_KW + GLA_WIDTH].astype(F32)
        v_t = jnp.concatenate([jnp.zeros((npad, GLA_WIDTH), F32), v_m], axis=0).T.astype(BF16)
        for h in range(GLA_HEADS):
            st_ref[h] = _dot(v_t[h * GLA_DV:(h + 1) * GLA_DV, :],
                             k_end[:, h * GLA_DK:(h + 1) * GLA_DK])
        ext_ref[0:POOL_HIST, :] = mp_ref[:, 2 * GLA_KW + 2 * GLA_WIDTH:].astype(F32)

    logg = _log_gate(glr_ref[...], gw2_ref, gb_ref)
    row = lax.broadcasted_iota(jnp.int32, (tb, tb), 0)
    col = lax.broadcasted_iota(jnp.int32, (tb, tb), 1)
    same_chunk = (row // CHUNK) == (col // CHUNK)
    cum_sel = jnp.where(same_chunk & (col <= row), 1.0, 0.0).astype(BF16)
    tot_sel = jnp.where(same_chunk, 1.0, 0.0).astype(BF16)
    g_cum = _rows_matmul_f32(cum_sel, logg)
    g_tot = _rows_matmul_f32(tot_sel, logg)
    causal = (same_chunk & (col <= row))[0:PAIR, 0:PAIR]
    first = lax.broadcasted_iota(jnp.int32, (PAIR, 1), 0) < CHUNK

    v_t = v_ref[...].astype(F32).T.astype(BF16)
    gnw = gnw_ref[...]
    for p in range(tb // PAIR):
        rs = slice(p * PAIR, (p + 1) * PAIR)
        gc = g_cum[rs]
        gt = g_tot[rs]
        qf = q_ref[rs, :].astype(F32)
        kf = k_ref[rs, :].astype(F32)
        q_dec = (qf * (GLA_DK ** -0.5) * jnp.exp(gc)).astype(BF16)
        k_inv = (kf * jnp.exp(-gc)).astype(BF16)
        k_end = kf * jnp.exp(gt - gc)
        k_end0 = jnp.where(first, k_end, 0.0).astype(BF16)
        k_end1 = jnp.where(first, 0.0, k_end).astype(BF16)
        dec0 = jnp.exp(gt[0:1])
        dec1 = jnp.exp(gt[CHUNK:CHUNK + 1])
        for h in range(GLA_HEADS):
            ck = slice(h * GLA_DK, (h + 1) * GLA_DK)
            cv = slice(h * GLA_DV, (h + 1) * GLA_DV)
            scores = jnp.where(causal, _dot_nt(q_dec[:, ck], k_inv[:, ck]), 0.0).astype(BF16)
            o = _dot(scores, v_ref[rs, cv])
            st0 = st_ref[h]
            v_th = v_t[cv, rs]
            st1 = st0 * dec0[:, ck] + _dot(v_th, k_end0[:, ck])
            st_ref[h] = st1 * dec1[:, ck] + _dot(v_th, k_end1[:, ck])
            o = o + jnp.concatenate(
                [_dot_nt(q_dec[0:CHUNK, ck], st0.astype(BF16)),
                 _dot_nt(q_dec[CHUNK:PAIR, ck], st1.astype(BF16))], axis=0)
            o = o * _rms_scale(o) * gnw
            rf = r_ref[rs, cv].astype(F32)
            gate = rf * (1.0 / (1.0 + jnp.exp(-rf)))
            o_ref[rs, cv] = (o * gate).astype(o_ref.dtype)

    ext_ref[POOL_HIST:POOL_HIST + tb, :] = pu_ref[...].astype(F32)
    for g, w in enumerate(POOL_WINDOWS):
        cs = slice(g * POOL_GC, (g + 1) * POOL_GC)
        cur = ext_ref[POOL_HIST:POOL_HIST + tb, cs]
        acc = cur
        for j in range(1, w):
            acc = acc + ext_ref[POOL_HIST - j:POOL_HIST - j + tb, cs]
        y = (acc * (1.0 / w) - cur).astype(BF16)
        y = _dot(y, pw_ref[g]) * ps_ref[:, cs]
        o_ref[:, GLA_WIDTH + g * POOL_GC:GLA_WIDTH + (g + 1) * POOL_GC] = y.astype(o_ref.dtype)
    ext_ref[0:POOL_HIST, :] = ext_ref[tb:tb + POOL_HIST, :]


def _mixer(proj, glr, meta_proj, meta_glr, gw2, gb, gnw, pw, ps, batch, seq, tb):
    nt = seq // tb
    tok = lambda c: (lambda b, i: (b * nt + i, c))
    full = lambda shape: pl.BlockSpec(shape, lambda b, i: (0,) * len(shape))
    return pl.pallas_call(
        _mixer_kernel,
        grid=(batch, nt),
        in_specs=[
            pl.BlockSpec((tb, GLA_KW), tok(0)),
            pl.BlockSpec((tb, GLA_KW), tok(1)),
            pl.BlockSpec((tb, GLA_WIDTH), tok(1)),
            pl.BlockSpec((tb, GLA_WIDTH), tok(2)),
            pl.BlockSpec((tb, POOL_WIDTH), tok(3)),
            pl.BlockSpec((tb, LANES), tok(0)),
            full(meta_proj.shape),
            full(meta_glr.shape),
            full(gw2.shape),
            full(gb.shape),
            full(gnw.shape),
            full(pw.shape),
            full(ps.shape),
        ],
        out_specs=pl.BlockSpec((tb, GLA_WIDTH + POOL_WIDTH), tok(0)),
        out_shape=jax.ShapeDtypeStruct((batch * seq, GLA_WIDTH + POOL_WIDTH), BF16),
        scratch_shapes=[
            pltpu.VMEM((GLA_HEADS, GLA_DV, GLA_DK), F32),
            pltpu.VMEM((tb + POOL_HIST, POOL_WIDTH), F32),
        ],
        compiler_params=pltpu.CompilerParams(
            dimension_semantics=("parallel", "arbitrary"),
            vmem_limit_bytes=VMEM_LIMIT),
        name="mixer",
    )(proj, proj, proj, proj, proj, glr, meta_proj, meta_glr, gw2, gb, gnw, pw, ps)


def _wout_kernel(m_ref, w_ref, x_ref, o_ref):
    o_ref[...] = x_ref[...] + _dot(m_ref[...], w_ref[...])


def _wout(mixed, w, x2d, tm):
    m, kdim = mixed.shape
    n = w.shape[1]
    return pl.pallas_call(
        _wout_kernel,
        grid=(m // tm,),
        in_specs=[
            pl.BlockSpec((tm, kdim), lambda i: (i, 0)),
            pl.BlockSpec((kdim, n), lambda i: (0, 0), pipeline_mode=pl.Buffered(1)),
            pl.BlockSpec((tm, n), lambda i: (i, 0)),
        ],
        out_specs=pl.BlockSpec((tm, n), lambda i: (i, 0)),
        out_shape=jax.ShapeDtypeStruct((m, n), F32),
        compiler_params=pltpu.CompilerParams(
            dimension_semantics=("parallel",), vmem_limit_bytes=VMEM_LIMIT),
        name="wout",
    )(mixed, w, x2d)


def _mlp_kernel(h_hbm, nw_ref, w1_ref, w2_ref, fnw_ref, o_ref, xn_ref, hbuf_ref, sem):
    m = pl.program_id(0)
    f = pl.program_id(1)
    tm = hbuf_ref.shape[0]

    def fetch(tile):
        return pltpu.make_async_copy(h_hbm.at[pl.ds(tile * tm, tm), :], hbuf_ref, sem)

    @pl.when((m == 0) & (f == 0))
    def _():
        fetch(0).start()

    @pl.when(f == 0)
    def _():
        fetch(m).wait()
        h = hbuf_ref[...]
        xn_ref[...] = (h * _rms_scale(h) * nw_ref[...]).astype(BF16)
        o_ref[...] = h

    @pl.when((f == 1) & (m + 1 < pl.num_programs(0)))
    def _():
        fetch(m + 1).start()

    z = _dot(xn_ref[...], w1_ref[...].astype(BF16))
    a = jnp.square(jnp.maximum(z, 0.0)).astype(BF16)
    o_ref[...] += _dot(a, w2_ref[...].astype(BF16))

    @pl.when(f == pl.num_programs(1) - 1)
    def _():
        y = o_ref[...]
        o_ref[...] = y * _rms_scale(y) * fnw_ref[...]


def _mlp(h, nw, w1, w2, fnw, tm, tf):
    m, d = h.shape
    ff = w1.shape[1]
    return pl.pallas_call(
        _mlp_kernel,
        grid=(m // tm, ff // tf),
        in_specs=[
            pl.BlockSpec(memory_space=pl.ANY),
            pl.BlockSpec((1, d), lambda i, j: (0, 0)),
            pl.BlockSpec((d, tf), lambda i, j: (0, j)),
            pl.BlockSpec((tf, d), lambda i, j: (j, 0)),
            pl.BlockSpec((1, d), lambda i, j: (0, 0)),
        ],
        out_specs=pl.BlockSpec((tm, d), lambda i, j: (i, 0)),
        out_shape=jax.ShapeDtypeStruct((m, d), F32),
        scratch_shapes=[
            pltpu.VMEM((tm, d), BF16),
            pltpu.VMEM((tm, d), F32),
            pltpu.SemaphoreType.DMA(()),
        ],
        compiler_params=pltpu.CompilerParams(
            dimension_semantics=("arbitrary", "arbitrary"),
            vmem_limit_bytes=VMEM_LIMIT),
        name="mlp",
    )(h, nw, w1, w2, fnw)


def kernel(x, meta_tokens, norm1_w, w_in, gate_w2, gate_b, gla_norm_w, pool_w, pool_scale,
           w_out, norm2_w, mlp_w1, mlp_w2, final_norm_w):
    batch, seq, d = x.shape
    assert w_in.shape[0] == 1, "single-layer trunk"
    assert meta_tokens.shape[0] == N_META

    wm, wg = _win_prep(w_in, PREP_TR)
    gw2 = jnp.pad(gate_w2[0], ((0, LANES - GATE_RANK), (0, 0))).astype(BF16)
    gb = gate_b[0].reshape(1, GLA_KW)
    gnw = gla_norm_w[0].reshape(1, GLA_DV)
    pw = pool_w[0].astype(BF16)
    ps = pool_scale[0].reshape(1, POOL_WIDTH)
    nw1 = norm1_w[0].reshape(1, d)
    nw2 = norm2_w[0].reshape(1, d)
    fnw = final_norm_w.reshape(1, d)

    x2d = x.reshape(batch * seq, d)
    proj, glr = _proj(x2d, nw1, wm, wg, PROJ_TM, PROJ_TN)
    meta_proj, meta_glr = _proj(meta_tokens.astype(x.dtype), nw1, wm, wg, N_META, PROJ_TN)
    mixed = _mixer(proj, glr, meta_proj, meta_glr, gw2, gb, gnw, pw, ps, batch, seq, MIX_TB)
    h1 = _wout(mixed, w_out[0].astype(BF16), x2d, WOUT_TM)
    out = _mlp(h1, nw2, mlp_w1[0], mlp_w2[0], fnw, MLP_TM, MLP_TF)
    return out.reshape(batch, seq, d)
```

```python
import functools

import jax
import jax.numpy as jnp
from jax import lax
from jax.experimental import pallas as pl
from jax.experimental.pallas import tpu as pltpu

F32 = jnp.float32
BF16 = jnp.bfloat16

N_META = 16
CHUNK = 64
PAIR = 2 * CHUNK
GLA_HEADS = 4
GLA_DK = 128
GLA_DV = 256
GLA_KW = GLA_HEADS * GLA_DK
GLA_WIDTH = GLA_HEADS * GLA_DV
GATE_RANK = 16
GATE_TAU = 16.0
POOL_WINDOWS = (2, 4, 8, 16)
POOL_GC = 256
POOL_WIDTH = POOL_GC * len(POOL_WINDOWS)
POOL_HIST = 16
EPS = 1e-6
LANES = 128
VMEM_LIMIT = 56 * 1024 * 1024

PROJ_TM, PROJ_TN = 1024, 1024
MIX_TB = 256
WOUT_TM = 1024
MLP_TM, MLP_TF = 1024, 512


def _rms_scale(x):
    return lax.rsqrt(jnp.mean(x * x, axis=-1, keepdims=True) + EPS)


def _dot(a, b):
    return jnp.dot(a, b, preferred_element_type=F32)


def _dot_nt(a, b):
    return lax.dot_general(a, b, (((1,), (1,)), ((), ())), preferred_element_type=F32)


def _proj_kernel(x_ref, nw_ref, wt_ref, wgt_ref, o_ref, g_ref, xn_ref):
    @pl.when(pl.program_id(1) == 0)
    def _():
        x = x_ref[...]
        xn = (x * _rms_scale(x) * nw_ref[...]).astype(BF16)
        xn_ref[...] = xn
        g_ref[...] = _dot_nt(xn, wgt_ref[...].astype(BF16))

    o_ref[...] = _dot_nt(xn_ref[...], wt_ref[...].astype(BF16)).astype(o_ref.dtype)


def _proj(x2d, nw, w_in_t, tm, tn):
    m, d = x2d.shape
    glr_lo = 2 * GLA_KW + 2 * GLA_WIDTH
    n = w_in_t.shape[0] - GATE_RANK
    assert glr_lo % tn == 0
    return pl.pallas_call(
        _proj_kernel,
        grid=(m // tm, n // tn),
        in_specs=[
            pl.BlockSpec((tm, d), lambda i, j: (i, 0)),
            pl.BlockSpec((1, d), lambda i, j: (0, 0)),
            pl.BlockSpec((pl.Element(tn), pl.Element(d)),
                         lambda i, j: (pl.multiple_of(
                             j * tn + jnp.where(j * tn >= glr_lo, GATE_RANK, 0), GATE_RANK), 0)),
            pl.BlockSpec((pl.Element(LANES), pl.Element(d)), lambda i, j: (glr_lo, 0)),
        ],
        out_specs=[
            pl.BlockSpec((tm, tn), lambda i, j: (i, j)),
            pl.BlockSpec((tm, LANES), lambda i, j: (i, 0)),
        ],
        out_shape=[
            jax.ShapeDtypeStruct((m, n), BF16),
            jax.ShapeDtypeStruct((m, LANES), F32),
        ],
        scratch_shapes=[pltpu.VMEM((tm, d), BF16)],
        compiler_params=pltpu.CompilerParams(
            dimension_semantics=("parallel", "arbitrary"),
            vmem_limit_bytes=VMEM_LIMIT),
        name="proj",
    )(x2d, nw, w_in_t, w_in_t)


def _log_gate(glr, gw2_ref, gb_ref):
    g = _dot(glr.astype(BF16), gw2_ref[...]) + gb_ref[...]
    return (jnp.minimum(g, 0.0) - jnp.log(1.0 + jnp.exp(-jnp.abs(g)))) * (1.0 / GATE_TAU)


def _rows_matmul_f32(sel_bf16, x):
    hi = x.astype(BF16)
    lo = (x - hi.astype(F32)).astype(BF16)
    return _dot(sel_bf16, hi) + _dot(sel_bf16, lo)


def _mixer_kernel(q_ref, k_ref, v_ref, r_ref, pu_ref, glr_ref, mp_ref, mg_ref,
                  gw2_ref, gb_ref, gnw_ref, pw_ref, ps_ref,
                  o_ref, st_ref, ext_ref):
    tb = q_ref.shape[0]

    @pl.when(pl.program_id(1) == 0)
    def _():
        npad = PAIR - N_META
        logg = jnp.concatenate(
            [jnp.zeros((npad, GLA_KW), F32), _log_gate(mg_ref[...], gw2_ref, gb_ref)], axis=0)
        row = lax.broadcasted_iota(jnp.int32, (PAIR, PAIR), 0)
        col = lax.broadcasted_iota(jnp.int32, (PAIR, PAIR), 1)
        suffix = jnp.where(col > row, 1.0, 0.0).astype(BF16)
        to_end = _rows_matmul_f32(suffix, logg)
        k_m = mp_ref[:, GLA_KW:2 * GLA_KW].astype(F32)
        k_end = (jnp.concatenate([jnp.zeros((npad, GLA_KW), F32), k_m], axis=0)
                 * jnp.exp(to_end)).astype(BF16)
        v_m = mp_ref[:, 2 * GLA_KW:2 * GLA_KW + GLA_WIDTH].astype(F32)
        v_t = jnp.concatenate([jnp.zeros((npad, GLA_WIDTH), F32), v_m], axis=0).T.astype(BF16)
        for h in range(GLA_HEADS):
            st_ref[h] = _dot(v_t[h * GLA_DV:(h + 1) * GLA_DV, :],
                             k_end[:, h * GLA_DK:(h + 1) * GLA_DK])
        ext_ref[0:POOL_HIST, :] = mp_ref[:, 2 * GLA_KW + 2 * GLA_WIDTH:].astype(F32)

    logg = _log_gate(glr_ref[...], gw2_ref, gb_ref)
    row = lax.broadcasted_iota(jnp.int32, (tb, tb), 0)
    col = lax.broadcasted_iota(jnp.int32, (tb, tb), 1)
    same_chunk = (row // CHUNK) == (col // CHUNK)
    cum_sel = jnp.where(same_chunk & (col <= row), 1.0, 0.0).astype(BF16)
    tot_sel = jnp.where(same_chunk, 1.0, 0.0).astype(BF16)
    g_cum = _rows_matmul_f32(cum_sel, logg)
    g_tot = _rows_matmul_f32(tot_sel, logg)
    causal = (same_chunk & (col <= row))[0:PAIR, 0:PAIR]
    first = lax.broadcasted_iota(jnp.int32, (PAIR, 1), 0) < CHUNK

    v_t = v_ref[...].astype(F32).T.astype(BF16)
    gnw = gnw_ref[...]
    for p in range(tb // PAIR):
        rs = slice(p * PAIR, (p + 1) * PAIR)
        gc = g_cum[rs]
        gt = g_tot[rs]
        qf = q_ref[rs, :].astype(F32)
        kf = k_ref[rs, :].astype(F32)
        q_dec = (qf * (GLA_DK ** -0.5) * jnp.exp(gc)).astype(BF16)
        k_inv = (kf * jnp.exp(-gc)).astype(BF16)
        k_end = kf * jnp.exp(gt - gc)
        k_end0 = jnp.where(first, k_end, 0.0).astype(BF16)
        k_end1 = jnp.where(first, 0.0, k_end).astype(BF16)
        dec0 = jnp.exp(gt[0:1])
        dec1 = jnp.exp(gt[CHUNK:CHUNK + 1])
        for h in range(GLA_HEADS):
            ck = slice(h * GLA_DK, (h + 1) * GLA_DK)
            cv = slice(h * GLA_DV, (h + 1) * GLA_DV)
            scores = jnp.where(causal, _dot_nt(q_dec[:, ck], k_inv[:, ck]), 0.0).astype(BF16)
            o = _dot(scores, v_ref[rs, cv])
            st0 = st_ref[h]
            v_th = v_t[cv, rs]
            st1 = st0 * dec0[:, ck] + _dot(v_th, k_end0[:, ck])
            st_ref[h] = st1 * dec1[:, ck] + _dot(v_th, k_end1[:, ck])
            o = o + jnp.concatenate(
                [_dot_nt(q_dec[0:CHUNK, ck], st0.astype(BF16)),
                 _dot_nt(q_dec[CHUNK:PAIR, ck], st1.astype(BF16))], axis=0)
            o = o * _rms_scale(o) * gnw
            rf = r_ref[rs, cv].astype(F32)
            gate = rf * (1.0 / (1.0 + jnp.exp(-rf)))
            o_ref[rs, cv] = (o * gate).astype(o_ref.dtype)

    ext_ref[POOL_HIST:POOL_HIST + tb, :] = pu_ref[...].astype(F32)
    for g, w in enumerate(POOL_WINDOWS):
        cs = slice(g * POOL_GC, (g + 1) * POOL_GC)
        cur = ext_ref[POOL_HIST:POOL_HIST + tb, cs]
        acc = cur
        for j in range(1, w):
            acc = acc + ext_ref[POOL_HIST - j:POOL_HIST - j + tb, cs]
        y = (acc * (1.0 / w) - cur).astype(BF16)
        y = _dot(y, pw_ref[g]) * ps_ref[:, cs]
        o_ref[:, GLA_WIDTH + g * POOL_GC:GLA_WIDTH + (g + 1) * POOL_GC] = y.astype(o_ref.dtype)
    ext_ref[0:POOL_HIST, :] = ext_ref[tb:tb + POOL_HIST, :]


def _mixer(proj, glr, meta_proj, meta_glr, gw2, gb, gnw, pw, ps, batch, seq, tb):
    nt = seq // tb
    tok = lambda c: (lambda b, i: (b * nt + i, c))
    full = lambda shape: pl.BlockSpec(shape, lambda b, i: (0,) * len(shape))
    return pl.pallas_call(
        _mixer_kernel,
        grid=(batch, nt),
        in_specs=[
            pl.BlockSpec((tb, GLA_KW), tok(0)),
            pl.BlockSpec((tb, GLA_KW), tok(1)),
            pl.BlockSpec((tb, GLA_WIDTH), tok(1)),
            pl.BlockSpec((tb, GLA_WIDTH), tok(2)),
            pl.BlockSpec((tb, POOL_WIDTH), tok(3)),
            pl.BlockSpec((tb, LANES), tok(0)),
            full(meta_proj.shape),
            full(meta_glr.shape),
            full(gw2.shape),
            full(gb.shape),
            full(gnw.shape),
            full(pw.shape),
            full(ps.shape),
        ],
        out_specs=pl.BlockSpec((tb, GLA_WIDTH + POOL_WIDTH), tok(0)),
        out_shape=jax.ShapeDtypeStruct((batch * seq, GLA_WIDTH + POOL_WIDTH), BF16),
        scratch_shapes=[
            pltpu.VMEM((GLA_HEADS, GLA_DV, GLA_DK), F32),
            pltpu.VMEM((tb + POOL_HIST, POOL_WIDTH), F32),
        ],
        compiler_params=pltpu.CompilerParams(
            dimension_semantics=("parallel", "arbitrary"),
            vmem_limit_bytes=VMEM_LIMIT),
        name="mixer",
    )(proj, proj, proj, proj, proj, glr, meta_proj, meta_glr, gw2, gb, gnw, pw, ps)


def _wout_kernel(m_ref, w_ref, x_ref, o_ref):
    o_ref[...] = x_ref[...] + _dot(m_ref[...], w_ref[...])


def _wout(mixed, w, x2d, tm):
    m, kdim = mixed.shape
    n = w.shape[1]
    return pl.pallas_call(
        _wout_kernel,
        grid=(m // tm,),
        in_specs=[
            pl.BlockSpec((tm, kdim), lambda i: (i, 0)),
            pl.BlockSpec((kdim, n), lambda i: (0, 0), pipeline_mode=pl.Buffered(1)),
            pl.BlockSpec((tm, n), lambda i: (i, 0)),
        ],
        out_specs=pl.BlockSpec((tm, n), lambda i: (i, 0)),
        out_shape=jax.ShapeDtypeStruct((m, n), F32),
        compiler_params=pltpu.CompilerParams(
            dimension_semantics=("parallel",), vmem_limit_bytes=VMEM_LIMIT),
        name="wout",
    )(mixed, w, x2d)


def _mlp_kernel(h_hbm, nw_ref, w1_ref, w2_ref, fnw_ref, o_ref, xn_ref, hbuf_ref, sem):
    m = pl.program_id(0)
    f = pl.program_id(1)
    tm = hbuf_ref.shape[0]

    def fetch(tile):
        return pltpu.make_async_copy(h_hbm.at[pl.ds(tile * tm, tm), :], hbuf_ref, sem)

    @pl.when((m == 0) & (f == 0))
    def _():
        fetch(0).start()

    @pl.when(f == 0)
    def _():
        fetch(m).wait()
        h = hbuf_ref[...]
        xn_ref[...] = (h * _rms_scale(h) * nw_ref[...]).astype(BF16)
        o_ref[...] = h

    @pl.when((f == 1) & (m + 1 < pl.num_programs(0)))
    def _():
        fetch(m + 1).start()

    z = _dot(xn_ref[...], w1_ref[...].astype(BF16))
    a = jnp.square(jnp.maximum(z, 0.0)).astype(BF16)
    o_ref[...] += _dot(a, w2_ref[...].astype(BF16))

    @pl.when(f == pl.num_programs(1) - 1)
    def _():
        y = o_ref[...]
        o_ref[...] = y * _rms_scale(y) * fnw_ref[...]


def _mlp(h, nw, w1, w2, fnw, tm, tf):
    m, d = h.shape
    ff = w1.shape[1]
    return pl.pallas_call(
        _mlp_kernel,
        grid=(m // tm, ff // tf),
        in_specs=[
            pl.BlockSpec(memory_space=pl.ANY),
            pl.BlockSpec((1, d), lambda i, j: (0, 0)),
            pl.BlockSpec((d, tf), lambda i, j: (0, j)),
            pl.BlockSpec((tf, d), lambda i, j: (j, 0)),
            pl.BlockSpec((1, d), lambda i, j: (0, 0)),
        ],
        out_specs=pl.BlockSpec((tm, d), lambda i, j: (i, 0)),
        out_shape=jax.ShapeDtypeStruct((m, d), F32),
        scratch_shapes=[
            pltpu.VMEM((tm, d), BF16),
            pltpu.VMEM((tm, d), F32),
            pltpu.SemaphoreType.DMA(()),
        ],
        compiler_params=pltpu.CompilerParams(
            dimension_semantics=("arbitrary", "arbitrary"),
            vmem_limit_bytes=VMEM_LIMIT),
        name="mlp",
    )(h, nw, w1, w2, fnw)


def kernel(x, meta_tokens, norm1_w, w_in, gate_w2, gate_b, gla_norm_w, pool_w, pool_scale,
           w_out, norm2_w, mlp_w1, mlp_w2, final_norm_w):
    batch, seq, d = x.shape
    assert w_in.shape[0] == 1, "single-layer trunk"
    assert meta_tokens.shape[0] == N_META

    w_in_t = w_in[0].T
    gw2 = jnp.pad(gate_w2[0], ((0, LANES - GATE_RANK), (0, 0))).astype(BF16)
    gb = gate_b[0].reshape(1, GLA_KW)
    gnw = gla_norm_w[0].reshape(1, GLA_DV)
    pw = pool_w[0].astype(BF16)
    ps = pool_scale[0].reshape(1, POOL_WIDTH)
    nw1 = norm1_w[0].reshape(1, d)
    nw2 = norm2_w[0].reshape(1, d)
    fnw = final_norm_w.reshape(1, d)

    x2d = x.reshape(batch * seq, d)
    proj, glr = _proj(x2d, nw1, w_in_t, PROJ_TM, PROJ_TN)
    meta_proj, meta_glr = _proj(meta_tokens.astype(x.dtype), nw1, w_in_t, N_META, PROJ_TN)
    mixed = _mixer(proj, glr, meta_proj, meta_glr, gw2, gb, gnw, pw, ps, batch, seq, MIX_TB)
    h1 = _wout(mixed, w_out[0].astype(BF16), x2d, WOUT_TM)
    out = _mlp(h1, nw2, mlp_w1[0], mlp_w2[0], fnw, MLP_TM, MLP_TF)
    return out.reshape(batch, seq, d)
```

```python
import functools

import jax
import jax.numpy as jnp
from jax import lax
from jax.experimental import pallas as pl
from jax.experimental.pallas import tpu as pltpu

F32 = jnp.float32
BF16 = jnp.bfloat16

N_META = 16
CHUNK = 64
PAIR = 2 * CHUNK
GLA_HEADS = 4
GLA_DK = 128
GLA_DV = 256
GLA_KW = GLA_HEADS * GLA_DK
GLA_WIDTH = GLA_HEADS * GLA_DV
GATE_RANK = 16
GATE_TAU = 16.0
POOL_WINDOWS = (2, 4, 8, 16)
POOL_GC = 256
POOL_WIDTH = POOL_GC * len(POOL_WINDOWS)
POOL_HIST = 16
EPS = 1e-6
LANES = 128
VMEM_LIMIT = 56 * 1024 * 1024

PROJ_TM, PROJ_TN = 1024, 1024
MIX_TB = 256
MLP_TM, MLP_TF = 1024, 512


def _rms_scale(x):
    return lax.rsqrt(jnp.mean(x * x, axis=-1, keepdims=True) + EPS)


def _dot(a, b):
    return jnp.dot(a, b, preferred_element_type=F32)


def _dot_nt(a, b):
    return lax.dot_general(a, b, (((1,), (1,)), ((), ())), preferred_element_type=F32)


def _dot_tn(a, b):
    return lax.dot_general(a, b, (((0,), (0,)), ((), ())), preferred_element_type=F32)


def _proj_kernel(x_ref, nw_ref, wt_ref, wgt_ref, o_ref, g_ref, xn_ref):
    @pl.when(pl.program_id(1) == 0)
    def _():
        x = x_ref[...]
        xn = (x * _rms_scale(x) * nw_ref[...]).astype(BF16)
        xn_ref[...] = xn
        g_ref[...] = _dot_nt(xn, wgt_ref[...].astype(BF16))

    o_ref[...] = _dot_nt(xn_ref[...], wt_ref[...].astype(BF16)).astype(o_ref.dtype)


def _proj(x2d, nw, w_in_t, tm, tn):
    m, d = x2d.shape
    glr_lo = 2 * GLA_KW + 2 * GLA_WIDTH
    n = w_in_t.shape[0] - GATE_RANK
    assert glr_lo % tn == 0
    return pl.pallas_call(
        _proj_kernel,
        grid=(m // tm, n // tn),
        in_specs=[
            pl.BlockSpec((tm, d), lambda i, j: (i, 0)),
            pl.BlockSpec((1, d), lambda i, j: (0, 0)),
            pl.BlockSpec((pl.Element(tn), pl.Element(d)),
                         lambda i, j: (pl.multiple_of(
                             j * tn + jnp.where(j * tn >= glr_lo, GATE_RANK, 0), GATE_RANK), 0)),
            pl.BlockSpec((pl.Element(LANES), pl.Element(d)), lambda i, j: (glr_lo, 0)),
        ],
        out_specs=[
            pl.BlockSpec((tm, tn), lambda i, j: (i, j)),
            pl.BlockSpec((tm, LANES), lambda i, j: (i, 0)),
        ],
        out_shape=[
            jax.ShapeDtypeStruct((m, n), BF16),
            jax.ShapeDtypeStruct((m, LANES), F32),
        ],
        scratch_shapes=[pltpu.VMEM((tm, d), BF16)],
        compiler_params=pltpu.CompilerParams(
            dimension_semantics=("parallel", "arbitrary"),
            vmem_limit_bytes=VMEM_LIMIT),
        name="proj",
    )(x2d, nw, w_in_t, w_in_t)


def _log_gate(glr, gw2_ref, gb_ref):
    g = _dot(glr.astype(BF16), gw2_ref[...]) + gb_ref[...]
    return (jnp.minimum(g, 0.0) - jnp.log(1.0 + jnp.exp(-jnp.abs(g)))) * (1.0 / GATE_TAU)


def _rows_matmul_f32(sel_bf16, x):
    hi = x.astype(BF16)
    lo = (x - hi.astype(F32)).astype(BF16)
    return _dot(sel_bf16, hi) + _dot(sel_bf16, lo)


def _mixer_kernel(nt, q_ref, k_ref, v_ref, r_ref, pu_ref, glr_ref, mp_ref, mg_ref,
                  gw2_ref, gb_ref, gnw_ref, pw_ref, ps_ref, bandc_ref, bandh_ref, w_ref, x_ref,
                  h_ref, st_ref, hist_ref, mixed0_ref, mixed1_ref, wbf_ref):
    s = pl.program_id(0)

    @pl.when(s == 0)
    def _():
        wbf_ref[...] = w_ref[...].astype(BF16)
        mixed1_ref[...] = jnp.zeros(mixed1_ref.shape, mixed1_ref.dtype)

    @pl.when(lax.rem(s, nt) == 0)
    def _():
        npad = PAIR - N_META
        logg = jnp.concatenate(
            [jnp.zeros((npad, GLA_KW), F32), _log_gate(mg_ref[...], gw2_ref, gb_ref)], axis=0)
        row = lax.broadcasted_iota(jnp.int32, (PAIR, PAIR), 0)
        col = lax.broadcasted_iota(jnp.int32, (PAIR, PAIR), 1)
        suffix = jnp.where(col > row, 1.0, 0.0).astype(BF16)
        to_end = _rows_matmul_f32(suffix, logg)
        k_m = mp_ref[:, GLA_KW:2 * GLA_KW].astype(F32)
        k_end = (jnp.concatenate([jnp.zeros((npad, GLA_KW), F32), k_m], axis=0)
                 * jnp.exp(to_end)).astype(BF16)
        v_m = mp_ref[:, 2 * GLA_KW:2 * GLA_KW + GLA_WIDTH].astype(F32)
        v_t = jnp.concatenate([jnp.zeros((npad, GLA_WIDTH), F32), v_m], axis=0).T.astype(BF16)
        for h in range(GLA_HEADS):
            st_ref[h] = _dot(v_t[h * GLA_DV:(h + 1) * GLA_DV, :],
                             k_end[:, h * GLA_DK:(h + 1) * GLA_DK])
        hist_ref[...] = mp_ref[:, 2 * GLA_KW + 2 * GLA_WIDTH:]

    @pl.when(lax.rem(s, 2) == 0)
    def _():
        _mix_block(q_ref, k_ref, v_ref, r_ref, pu_ref, glr_ref, gw2_ref, gb_ref, gnw_ref, pw_ref,
                   ps_ref, bandc_ref, bandh_ref, st_ref, hist_ref, mixed0_ref)
        h_ref[...] = x_ref[...] + _dot(mixed1_ref[...], wbf_ref[...])

    @pl.when(lax.rem(s, 2) == 1)
    def _():
        _mix_block(q_ref, k_ref, v_ref, r_ref, pu_ref, glr_ref, gw2_ref, gb_ref, gnw_ref, pw_ref,
                   ps_ref, bandc_ref, bandh_ref, st_ref, hist_ref, mixed1_ref)
        h_ref[...] = x_ref[...] + _dot(mixed0_ref[...], wbf_ref[...])


def _mix_block(q_ref, k_ref, v_ref, r_ref, pu_ref, glr_ref, gw2_ref, gb_ref, gnw_ref, pw_ref,
               ps_ref, bandc_ref, bandh_ref, st_ref, hist_ref, o_ref):
    tb = q_ref.shape[0]

    logg = _log_gate(glr_ref[...], gw2_ref, gb_ref)
    row = lax.broadcasted_iota(jnp.int32, (tb, tb), 0)
    col = lax.broadcasted_iota(jnp.int32, (tb, tb), 1)
    same_chunk = (row // CHUNK) == (col // CHUNK)
    cum_sel = jnp.where(same_chunk & (col <= row), 1.0, 0.0).astype(BF16)
    g_cum = _rows_matmul_f32(cum_sel, logg)
    causal = (same_chunk & (col <= row))[0:PAIR, 0:PAIR]

    gnw = gnw_ref[...]
    for p in range(tb // PAIR):
        rs = slice(p * PAIR, (p + 1) * PAIR)
        r0 = slice(p * PAIR, p * PAIR + CHUNK)
        r1 = slice(p * PAIR + CHUNK, (p + 1) * PAIR)
        gc = g_cum[rs]
        gl0 = gc[CHUNK - 1:CHUNK]
        gl1 = gc[PAIR - 1:PAIR]
        gt = jnp.concatenate([jnp.broadcast_to(gl0, (CHUNK, GLA_KW)),
                              jnp.broadcast_to(gl1, (CHUNK, GLA_KW))], axis=0)
        qf = q_ref[rs, :].astype(F32)
        kf = k_ref[rs, :].astype(F32)
        q_dec = (qf * (GLA_DK ** -0.5) * jnp.exp(gc)).astype(BF16)
        k_inv = (kf * jnp.exp(-gc)).astype(BF16)
        k_end = (kf * jnp.exp(gt - gc)).astype(BF16)
        dec0 = jnp.exp(gl0)
        dec1 = jnp.exp(gl1)
        for h in range(GLA_HEADS):
            ck = slice(h * GLA_DK, (h + 1) * GLA_DK)
            cv = slice(h * GLA_DV, (h + 1) * GLA_DV)
            scores = jnp.where(causal, _dot_nt(q_dec[:, ck], k_inv[:, ck]), 0.0).astype(BF16)
            o = _dot(scores, v_ref[rs, cv])
            st0 = st_ref[h]
            st1 = st0 * dec0[:, ck] + _dot_tn(v_ref[r0, cv], k_end[0:CHUNK, ck])
            st_ref[h] = st1 * dec1[:, ck] + _dot_tn(v_ref[r1, cv], k_end[CHUNK:PAIR, ck])
            o = o + jnp.concatenate(
                [_dot_nt(q_dec[0:CHUNK, ck], st0.astype(BF16)),
                 _dot_nt(q_dec[CHUNK:PAIR, ck], st1.astype(BF16))], axis=0)
            o = o * _rms_scale(o) * gnw
            rf = r_ref[rs, cv].astype(F32)
            gate = rf * (1.0 / (1.0 + jnp.exp(-rf)))
            o_ref[rs, cv] = (o * gate).astype(o_ref.dtype)

    for g in range(len(POOL_WINDOWS)):
        cs = slice(g * POOL_GC, (g + 1) * POOL_GC)
        y = _dot(bandc_ref[g], pu_ref[:, cs])
        y_head = y[0:POOL_HIST] + _dot(bandh_ref[g], hist_ref[:, cs])
        y = jnp.concatenate([y_head, y[POOL_HIST:]], axis=0).astype(BF16)
        y = _dot(y, pw_ref[g]) * ps_ref[:, cs]
        o_ref[:, GLA_WIDTH + g * POOL_GC:GLA_WIDTH + (g + 1) * POOL_GC] = y.astype(o_ref.dtype)
    hist_ref[...] = pu_ref[tb - POOL_HIST:tb, :]


def _pool_bands(tb):
    t = jnp.arange(tb)[:, None] + POOL_HIST
    j = jnp.arange(tb + POOL_HIST)[None, :]
    mats = [jnp.where((j <= t) & (j > t - w), 1.0 / w, 0.0) - jnp.where(j == t, 1.0, 0.0)
            for w in POOL_WINDOWS]
    band = jnp.stack(mats).astype(BF16)
    return band[:, :, POOL_HIST:], band[:, :POOL_HIST, :POOL_HIST]


def _mixer(proj, glr, meta_proj, meta_glr, gw2, gb, gnw, pw, ps, w_out, x2d, batch, seq, tb):
    nt = seq // tb
    nb = batch * nt
    d = x2d.shape[1]
    bandc, bandh = _pool_bands(tb)
    tok = lambda c: (lambda s: (jnp.minimum(s, nb - 1), c))
    lag = lambda s: (jnp.maximum(s - 1, 0), 0)
    full = lambda shape: pl.BlockSpec(shape, lambda s: (0,) * len(shape))
    return pl.pallas_call(
        functools.partial(_mixer_kernel, nt),
        grid=(nb + 1,),
        in_specs=[
            pl.BlockSpec((tb, GLA_KW), tok(0)),
            pl.BlockSpec((tb, GLA_KW), tok(1)),
            pl.BlockSpec((tb, GLA_WIDTH), tok(1)),
            pl.BlockSpec((tb, GLA_WIDTH), tok(2)),
            pl.BlockSpec((tb, POOL_WIDTH), tok(3)),
            pl.BlockSpec((tb, LANES), tok(0)),
            full(meta_proj.shape),
            full(meta_glr.shape),
            full(gw2.shape),
            full(gb.shape),
            full(gnw.shape),
            full(pw.shape),
            full(ps.shape),
            full(bandc.shape),
            full(bandh.shape),
            pl.BlockSpec(w_out.shape, lambda s: (0, 0), pipeline_mode=pl.Buffered(1)),
            pl.BlockSpec((tb, d), lag),
        ],
        out_specs=pl.BlockSpec((tb, d), lag),
        out_shape=jax.ShapeDtypeStruct((batch * seq, d), F32),
        scratch_shapes=[
            pltpu.VMEM((GLA_HEADS, GLA_DV, GLA_DK), F32),
            pltpu.VMEM((POOL_HIST, POOL_WIDTH), BF16),
            pltpu.VMEM((tb, GLA_WIDTH + POOL_WIDTH), BF16),
            pltpu.VMEM((tb, GLA_WIDTH + POOL_WIDTH), BF16),
            pltpu.VMEM(w_out.shape, BF16),
        ],
        compiler_params=pltpu.CompilerParams(
            dimension_semantics=("arbitrary",), vmem_limit_bytes=VMEM_LIMIT),
        name="mixer",
    )(proj, proj, proj, proj, proj, glr, meta_proj, meta_glr, gw2, gb, gnw, pw, ps,
      bandc, bandh, w_out, x2d)


def _mlp_kernel(h_hbm, nw_ref, w1_ref, w2_ref, fnw_ref, o_ref, xn_ref, hbuf_ref, sem):
    m = pl.program_id(0)
    f = pl.program_id(1)
    tm = hbuf_ref.shape[0]

    def fetch(tile):
        return pltpu.make_async_copy(h_hbm.at[pl.ds(tile * tm, tm), :], hbuf_ref, sem)

    @pl.when((m == 0) & (f == 0))
    def _():
        fetch(0).start()

    @pl.when(f == 0)
    def _():
        fetch(m).wait()
        h = hbuf_ref[...]
        xn_ref[...] = (h * _rms_scale(h) * nw_ref[...]).astype(BF16)
        o_ref[...] = h

    @pl.when((f == 1) & (m + 1 < pl.num_programs(0)))
    def _():
        fetch(m + 1).start()

    z = _dot(xn_ref[...], w1_ref[...].astype(BF16))
    a = jnp.square(jnp.maximum(z, 0.0)).astype(BF16)
    o_ref[...] += _dot(a, w2_ref[...].astype(BF16))

    @pl.when(f == pl.num_programs(1) - 1)
    def _():
        y = o_ref[...]
        o_ref[...] = y * _rms_scale(y) * fnw_ref[...]


def _mlp(h, nw, w1, w2, fnw, tm, tf):
    m, d = h.shape
    ff = w1.shape[1]
    return pl.pallas_call(
        _mlp_kernel,
        grid=(m // tm, ff // tf),
        in_specs=[
            pl.BlockSpec(memory_space=pl.ANY),
            pl.BlockSpec((1, d), lambda i, j: (0, 0)),
            pl.BlockSpec((d, tf), lambda i, j: (0, j)),
            pl.BlockSpec((tf, d), lambda i, j: (j, 0)),
            pl.BlockSpec((1, d), lambda i, j: (0, 0)),
        ],
        out_specs=pl.BlockSpec((tm, d), lambda i, j: (i, 0)),
        out_shape=jax.ShapeDtypeStruct((m, d), F32),
        scratch_shapes=[
            pltpu.VMEM((tm, d), BF16),
            pltpu.VMEM((tm, d), F32),
            pltpu.SemaphoreType.DMA(()),
        ],
        compiler_params=pltpu.CompilerParams(
            dimension_semantics=("arbitrary", "arbitrary"),
            vmem_limit_bytes=VMEM_LIMIT),
        name="mlp",
    )(h, nw, w1, w2, fnw)


def kernel(x, meta_tokens, norm1_w, w_in, gate_w2, gate_b, gla_norm_w, pool_w, pool_scale,
           w_out, norm2_w, mlp_w1, mlp_w2, final_norm_w):
    batch, seq, d = x.shape
    assert w_in.shape[0] == 1, "single-layer trunk"
    assert meta_tokens.shape[0] == N_META

    w_in_t = w_in[0].T
    gw2 = jnp.pad(gate_w2[0], ((0, LANES - GATE_RANK), (0, 0))).astype(BF16)
    gb = gate_b[0].reshape(1, GLA_KW)
    gnw = gla_norm_w[0].reshape(1, GLA_DV)
    pw = pool_w[0].astype(BF16)
    ps = pool_scale[0].reshape(1, POOL_WIDTH)
    nw1 = norm1_w[0].reshape(1, d)
    nw2 = norm2_w[0].reshape(1, d)
    fnw = final_norm_w.reshape(1, d)

    x2d = x.reshape(batch * seq, d)
    proj, glr = _proj(x2d, nw1, w_in_t, PROJ_TM, PROJ_TN)
    meta_proj, meta_glr = _proj(meta_tokens.astype(x.dtype), nw1, w_in_t, N_META, PROJ_TN)
    h1 = _mixer(proj, glr, meta_proj, meta_glr, gw2, gb, gnw, pw, ps, w_out[0], x2d,
                batch, seq, MIX_TB)
    out = _mlp(h1, nw2, mlp_w1[0], mlp_w2[0], fnw, MLP_TM, MLP_TF)
    return out.reshape(batch, seq, d)
```

```python
import functools

import jax
import jax.numpy as jnp
from jax import lax
from jax.experimental import pallas as pl
from jax.experimental.pallas import tpu as pltpu

F32 = jnp.float32
BF16 = jnp.bfloat16

N_META = 16
CHUNK = 64
PAIR = 2 * CHUNK
GLA_HEADS = 4
GLA_DK = 128
GLA_DV = 256
GLA_KW = GLA_HEADS * GLA_DK
GLA_WIDTH = GLA_HEADS * GLA_DV
GATE_RANK = 16
GATE_TAU = 16.0
POOL_WINDOWS = (2, 4, 8, 16)
POOL_GC = 256
POOL_WIDTH = POOL_GC * len(POOL_WINDOWS)
POOL_HIST = 16
MIX_WIDTH = GLA_WIDTH + POOL_WIDTH
COL_Q, COL_K, COL_V = 0, GLA_KW, 2 * GLA_KW
COL_R = COL_V + GLA_WIDTH
COL_PU = COL_R + GLA_WIDTH
PROJ_WIDTH = COL_PU + POOL_WIDTH
EPS = 1e-6
LANES = 128
VMEM_LIMIT = 56 * 1024 * 1024

FRONT_TB = 256
PREP_ROWS = 512
MLP_TM, MLP_TF = 1024, 512


def _rms_scale(x):
    return lax.rsqrt(jnp.mean(x * x, axis=-1, keepdims=True) + EPS)


def _dot(a, b):
    return jnp.dot(a, b, preferred_element_type=F32)


def _dot_nt(a, b):
    return lax.dot_general(a, b, (((1,), (1,)), ((), ())), preferred_element_type=F32)


def _dot_tn(a, b):
    return lax.dot_general(a, b, (((0,), (0,)), ((), ())), preferred_element_type=F32)


def _log_gate(glr, gw2_ref, gb_ref):
    g = _dot(glr.astype(BF16), gw2_ref[...]) + gb_ref[...]
    return (jnp.minimum(g, 0.0) - jnp.log(1.0 + jnp.exp(-jnp.abs(g)))) * (1.0 / GATE_TAU)


def _rows_matmul_f32(sel_bf16, x):
    hi = x.astype(BF16)
    lo = (x - hi.astype(F32)).astype(BF16)
    return _dot(sel_bf16, hi) + _dot(sel_bf16, lo)


def _prepare_weights(wint_hbm, wout_hbm, stage_ref, sem, wm_ref, wg_ref, wo_ref):
    rows = stage_ref.shape[1]
    n_win = PROJ_WIDTH // rows
    n_wout = wout_hbm.shape[0] // rows

    def copy(src_hbm, start, slot):
        return pltpu.make_async_copy(src_hbm.at[pl.ds(start, rows), :], stage_ref.at[slot],
                                     sem.at[slot])

    def win_start(j):
        col = j * rows
        src = jnp.where(j < n_win, col + jnp.where(col >= COL_PU, GATE_RANK, 0), COL_PU)
        return pl.multiple_of(src, GATE_RANK)

    def staged_t(slot):
        return stage_ref[slot].T.astype(BF16)

    copy(wint_hbm, win_start(0), 0).start()

    def win_pair(i, carry):
        copy(wint_hbm, win_start(2 * i + 1), 1).start()
        copy(wint_hbm, 0, 0).wait()
        wm_ref[i, :, 0:rows] = staged_t(0)
        copy(wint_hbm, win_start(2 * i + 2), 0).start()
        copy(wint_hbm, 0, 1).wait()
        wm_ref[i, :, rows:2 * rows] = staged_t(1)
        return carry

    lax.fori_loop(0, n_win // 2, win_pair, 0)
    copy(wout_hbm, 0, 1).start()
    copy(wint_hbm, 0, 0).wait()
    wg_ref[...] = staged_t(0)[:, 0:LANES]

    def wout_pair(i, carry):
        r_even = pl.multiple_of(2 * i * rows, rows)
        r_odd = pl.multiple_of((2 * i + 1) * rows, rows)
        copy(wout_hbm, r_odd, 0).start()
        copy(wout_hbm, 0, 1).wait()
        wo_ref[pl.ds(r_even, rows), :] = stage_ref[1].astype(BF16)

        @pl.when(2 * i + 2 < n_wout)
        def _():
            copy(wout_hbm, pl.multiple_of((2 * i + 2) * rows, rows), 1).start()

        copy(wout_hbm, 0, 0).wait()
        wo_ref[pl.ds(r_odd, rows), :] = stage_ref[0].astype(BF16)
        return carry

    lax.fori_loop(0, n_wout // 2, wout_pair, 0)


def _init_state(mp_ref, mg_ref, gw2_ref, gb_ref, st_ref, hist_ref):
    npad = PAIR - N_META
    logg = jnp.concatenate(
        [jnp.zeros((npad, GLA_KW), F32), _log_gate(mg_ref[...], gw2_ref, gb_ref)], axis=0)
    row = lax.broadcasted_iota(jnp.int32, (PAIR, PAIR), 0)
    col = lax.broadcasted_iota(jnp.int32, (PAIR, PAIR), 1)
    suffix = jnp.where(col > row, 1.0, 0.0).astype(BF16)
    to_end = _rows_matmul_f32(suffix, logg)
    k_m = mp_ref[:, COL_K:COL_K + GLA_KW].astype(F32)
    k_end = (jnp.concatenate([jnp.zeros((npad, GLA_KW), F32), k_m], axis=0)
             * jnp.exp(to_end)).astype(BF16)
    v_m = mp_ref[:, COL_V:COL_V + GLA_WIDTH].astype(F32)
    v_t = jnp.concatenate([jnp.zeros((npad, GLA_WIDTH), F32), v_m], axis=0).T.astype(BF16)
    for h in range(GLA_HEADS):
        st_ref[h] = _dot(v_t[h * GLA_DV:(h + 1) * GLA_DV, :],
                         k_end[:, h * GLA_DK:(h + 1) * GLA_DK])
    hist_ref[...] = mp_ref[:, COL_PU:COL_PU + POOL_WIDTH]


def _proj_norm(x_ref, nw_ref, wg_ref, xn_ref, g_ref):
    x = x_ref[...]
    xn = (x * _rms_scale(x) * nw_ref[...]).astype(BF16)
    xn_ref[...] = xn
    g_ref[...] = _dot(xn, wg_ref[...])


def _proj_chunk(c, xn_ref, wm_ref, p_ref):
    cw = wm_ref.shape[2]
    p_ref[:, c * cw:(c + 1) * cw] = _dot(xn_ref[...], wm_ref[c]).astype(p_ref.dtype)


def _mix_block(p_ref, g_ref, gw2_ref, gb_ref, gnw_ref, pw_ref, ps_ref, bandc_ref, bandh_ref,
               st_ref, hist_ref, o_ref, fillers):
    tb = p_ref.shape[0]
    fillers = list(fillers)

    def fill():
        if fillers:
            fillers.pop(0)()

    fill()
    logg = _log_gate(g_ref[...], gw2_ref, gb_ref)
    row = lax.broadcasted_iota(jnp.int32, (tb, tb), 0)
    col = lax.broadcasted_iota(jnp.int32, (tb, tb), 1)
    same_chunk = (row // CHUNK) == (col // CHUNK)
    cum_sel = jnp.where(same_chunk & (col <= row), 1.0, 0.0).astype(BF16)
    g_cum = _rows_matmul_f32(cum_sel, logg)
    causal = (same_chunk & (col <= row))[0:PAIR, 0:PAIR]

    n_pairs = tb // PAIR
    units = [(p, h) for p in range(n_pairs) for h in range(GLA_HEADS)]
    rows = lambda p, c: slice(p * PAIR + c * CHUNK, p * PAIR + (c + 1) * CHUNK)
    slab = lambda p: slice(p * PAIR, (p + 1) * PAIR)
    ck = lambda h: slice(h * GLA_DK, (h + 1) * GLA_DK)
    cv = lambda h: slice(COL_V + h * GLA_DV, COL_V + (h + 1) * GLA_DV)
    cr = lambda h: slice(COL_R + h * GLA_DV, COL_R + (h + 1) * GLA_DV)

    q_dec, k_inv, k_end, dec = [], [], [], []
    for p in range(n_pairs):
        gc = g_cum[slab(p)]
        gl = [gc[CHUNK - 1:CHUNK], gc[PAIR - 1:PAIR]]
        gt = jnp.concatenate([jnp.broadcast_to(g, (CHUNK, GLA_KW)) for g in gl], axis=0)
        qf = p_ref[slab(p), COL_Q:COL_Q + GLA_KW].astype(F32)
        kf = p_ref[slab(p), COL_K:COL_K + GLA_KW].astype(F32)
        q_dec.append((qf * (GLA_DK ** -0.5) * jnp.exp(gc)).astype(BF16))
        k_inv.append((kf * jnp.exp(-gc)).astype(BF16))
        k_end.append((kf * jnp.exp(gt - gc)).astype(BF16))
        dec.append([jnp.exp(g) for g in gl])

    fill()
    scores = {u: _dot_nt(q_dec[u[0]][:, ck(u[1])], k_inv[u[0]][:, ck(u[1])]) for u in units}
    d_st = {(p, h, c): _dot_tn(p_ref[rows(p, c), cv(h)],
                               k_end[p][c * CHUNK:(c + 1) * CHUNK, ck(h)])
            for (p, h) in units for c in range(2)}

    fill()
    scores = {u: jnp.where(causal, scores[u], 0.0).astype(BF16) for u in units}
    st_in = {}
    for h in range(GLA_HEADS):
        st = st_ref[h]
        for p in range(n_pairs):
            for c in range(2):
                st_in[(p, h, c)] = st.astype(BF16)
                st = st * dec[p][c][:, ck(h)] + d_st[(p, h, c)]
        st_ref[h] = st

    fill()
    outs = {}
    for (p, h) in units:
        o = _dot(scores[(p, h)], p_ref[slab(p), cv(h)])
        outs[(p, h)] = o + jnp.concatenate(
            [_dot_nt(q_dec[p][c * CHUNK:(c + 1) * CHUNK, ck(h)], st_in[(p, h, c)])
             for c in range(2)], axis=0)

    n_groups = len(POOL_WINDOWS)
    cs = lambda g: slice(g * POOL_GC, (g + 1) * POOL_GC)
    cp = lambda g: slice(COL_PU + g * POOL_GC, COL_PU + (g + 1) * POOL_GC)
    y_cur = [_dot(bandc_ref[g], p_ref[:, cp(g)]) for g in range(n_groups)]
    y_hist = [_dot(bandh_ref[g], hist_ref[:, cs(g)]) for g in range(n_groups)]
    hist_ref[...] = p_ref[tb - POOL_HIST:tb, COL_PU:COL_PU + POOL_WIDTH]
    y = [jnp.concatenate([y_cur[g][0:POOL_HIST] + y_hist[g], y_cur[g][POOL_HIST:]],
                         axis=0).astype(BF16) for g in range(n_groups)]
    y = [_dot(y[g], pw_ref[g]) for g in range(n_groups)]

    while fillers:
        fill()
    gnw = gnw_ref[...]
    for (p, h) in units:
        o = outs[(p, h)]
        o = o * _rms_scale(o) * gnw
        rf = p_ref[slab(p), cr(h)].astype(F32)
        gate = rf * (1.0 / (1.0 + jnp.exp(-rf)))
        o_ref[slab(p), h * GLA_DV:(h + 1) * GLA_DV] = (o * gate).astype(o_ref.dtype)
    for g in range(n_groups):
        o_ref[:, GLA_WIDTH + g * POOL_GC:GLA_WIDTH + (g + 1) * POOL_GC] = (
            y[g] * ps_ref[:, cs(g)]).astype(o_ref.dtype)


def _front_kernel(nt, x_ref, nw_ref, wint_hbm, wout_hbm, meta_ref,
                  gw2_ref, gb_ref, gnw_ref, pw_ref, ps_ref, bandc_ref, bandh_ref,
                  h_ref,
                  wm_ref, wg_ref, wo_ref, stage_ref, sem, mp_ref, mg_ref, st_ref, hist_ref,
                  xn_ref, p_ref, g_ref, m_ref):
    s = pl.program_id(0)

    @pl.when(s == 0)
    def _():
        _prepare_weights(wint_hbm, wout_hbm, stage_ref, sem, wm_ref, wg_ref, wo_ref)
        _proj_norm(meta_ref, nw_ref, wg_ref, xn_ref.at[pl.ds(0, N_META), :], mg_ref)
        for c in range(wm_ref.shape[0]):
            _proj_chunk(c, xn_ref.at[pl.ds(0, N_META), :], wm_ref, mp_ref)

    @pl.when(lax.rem(s, nt) == 0)
    def _():
        _init_state(mp_ref, mg_ref, gw2_ref, gb_ref, st_ref, hist_ref)

    _proj_norm(x_ref, nw_ref, wg_ref, xn_ref, g_ref)
    chunk = lambda c: functools.partial(_proj_chunk, c, xn_ref, wm_ref, p_ref)
    n_chunk = lambda col: col // wm_ref.shape[2]
    for c in range(n_chunk(COL_V)):
        chunk(c)()
    late = (list(range(n_chunk(COL_V), n_chunk(COL_R)))
            + list(range(n_chunk(COL_PU), n_chunk(PROJ_WIDTH)))
            + list(range(n_chunk(COL_R), n_chunk(COL_PU))))
    _mix_block(p_ref, g_ref, gw2_ref, gb_ref, gnw_ref, pw_ref, ps_ref, bandc_ref, bandh_ref,
               st_ref, hist_ref, m_ref, [chunk(c) for c in late])
    h_ref[...] = x_ref[...] + _dot(m_ref[...], wo_ref[...])


def _pool_bands(tb):
    t = jnp.arange(tb)[:, None] + POOL_HIST
    j = jnp.arange(tb + POOL_HIST)[None, :]
    mats = [jnp.where((j <= t) & (j > t - w), 1.0 / w, 0.0) - jnp.where(j == t, 1.0, 0.0)
            for w in POOL_WINDOWS]
    band = jnp.stack(mats).astype(BF16)
    return band[:, :, POOL_HIST:], band[:, :POOL_HIST, :POOL_HIST]


def _front(x2d, nw, w_in_t, w_out, meta, gw2, gb, gnw, pw, ps, batch, seq, tb):
    nt = seq // tb
    d = x2d.shape[1]
    assert w_in_t.shape == (PROJ_WIDTH + GATE_RANK, d) and PROJ_WIDTH % PREP_ROWS == 0
    assert all(c % PREP_ROWS == 0 for c in (COL_V, COL_R, COL_PU, w_out.shape[0]))
    bandc, bandh = _pool_bands(tb)
    full = lambda a: pl.BlockSpec(a.shape, lambda s: (0,) * a.ndim)
    return pl.pallas_call(
        functools.partial(_front_kernel, nt),
        grid=(batch * nt,),
        in_specs=[
            pl.BlockSpec((tb, d), lambda s: (s, 0)),
            full(nw),
            pl.BlockSpec(memory_space=pl.ANY),
            pl.BlockSpec(memory_space=pl.ANY),
            full(meta), full(gw2), full(gb), full(gnw), full(pw), full(ps),
            full(bandc), full(bandh),
        ],
        out_specs=pl.BlockSpec((tb, d), lambda s: (s, 0)),
        out_shape=jax.ShapeDtypeStruct((batch * seq, d), F32),
        scratch_shapes=[
            pltpu.VMEM((PROJ_WIDTH // PREP_ROWS, d, PREP_ROWS), BF16),
            pltpu.VMEM((d, LANES), BF16),
            pltpu.VMEM(w_out.shape, BF16),
            pltpu.VMEM((2, PREP_ROWS // 2, d), F32),
            pltpu.SemaphoreType.DMA((2,)),
            pltpu.VMEM((N_META, PROJ_WIDTH), BF16),
            pltpu.VMEM((N_META, LANES), F32),
            pltpu.VMEM((GLA_HEADS, GLA_DV, GLA_DK), F32),
            pltpu.VMEM((POOL_HIST, POOL_WIDTH), BF16),
            pltpu.VMEM((tb, d), BF16),
            pltpu.VMEM((tb, PROJ_WIDTH), BF16),
            pltpu.VMEM((tb, LANES), F32),
            pltpu.VMEM((tb, MIX_WIDTH), BF16),
        ],
        compiler_params=pltpu.CompilerParams(
            dimension_semantics=("arbitrary",), vmem_limit_bytes=VMEM_LIMIT),
        name="front",
    )(x2d, nw, w_in_t, w_out, meta, gw2, gb, gnw, pw, ps, bandc, bandh)


def _mlp_kernel(h_hbm, nw_ref, w1_ref, w2_ref, fnw_ref, o_ref, xn_ref, hbuf_ref, sem):
    m = pl.program_id(0)
    f = pl.program_id(1)
    tm = hbuf_ref.shape[0]

    def fetch(tile):
        return pltpu.make_async_copy(h_hbm.at[pl.ds(tile * tm, tm), :], hbuf_ref, sem)

    @pl.when((m == 0) & (f == 0))
    def _():
        fetch(0).start()

    @pl.when(f == 0)
    def _():
        fetch(m).wait()
        h = hbuf_ref[...]
        xn_ref[...] = (h * _rms_scale(h) * nw_ref[...]).astype(BF16)
        o_ref[...] = h

    @pl.when((f == 1) & (m + 1 < pl.num_programs(0)))
    def _():
        fetch(m + 1).start()

    z = _dot(xn_ref[...], w1_ref[...].astype(BF16))
    a = jnp.square(jnp.maximum(z, 0.0)).astype(BF16)
    o_ref[...] += _dot(a, w2_ref[...].astype(BF16))

    @pl.when(f == pl.num_programs(1) - 1)
    def _():
        y = o_ref[...]
        o_ref[...] = y * _rms_scale(y) * fnw_ref[...]


def _mlp(h, nw, w1, w2, fnw, tm, tf):
    m, d = h.shape
    ff = w1.shape[1]
    return pl.pallas_call(
        _mlp_kernel,
        grid=(m // tm, ff // tf),
        in_specs=[
            pl.BlockSpec(memory_space=pl.ANY),
            pl.BlockSpec((1, d), lambda i, j: (0, 0)),
            pl.BlockSpec((d, tf), lambda i, j: (0, j)),
            pl.BlockSpec((tf, d), lambda i, j: (j, 0)),
            pl.BlockSpec((1, d), lambda i, j: (0, 0)),
        ],
        out_specs=pl.BlockSpec((tm, d), lambda i, j: (i, 0)),
        out_shape=jax.ShapeDtypeStruct((m, d), F32),
        scratch_shapes=[
            pltpu.VMEM((tm, d), BF16),
            pltpu.VMEM((tm, d), F32),
            pltpu.SemaphoreType.DMA(()),
        ],
        compiler_params=pltpu.CompilerParams(
            dimension_semantics=("arbitrary", "arbitrary"),
            vmem_limit_bytes=VMEM_LIMIT),
        name="mlp",
    )(h, nw, w1, w2, fnw)


def kernel(x, meta_tokens, norm1_w, w_in, gate_w2, gate_b, gla_norm_w, pool_w, pool_scale,
           w_out, norm2_w, mlp_w1, mlp_w2, final_norm_w):
    batch, seq, d = x.shape
    assert w_in.shape[0] == 1, "single-layer trunk"
    assert meta_tokens.shape[0] == N_META

    w_in_t = w_in[0].T
    gw2 = jnp.pad(gate_w2[0], ((0, LANES - GATE_RANK), (0, 0))).astype(BF16)
    gb = gate_b[0].reshape(1, GLA_KW)
    gnw = gla_norm_w[0].reshape(1, GLA_DV)
    pw = pool_w[0].astype(BF16)
    ps = pool_scale[0].reshape(1, POOL_WIDTH)
    nw1 = norm1_w[0].reshape(1, d)
    nw2 = norm2_w[0].reshape(1, d)
    fnw = final_norm_w.reshape(1, d)

    x2d = x.reshape(batch * seq, d)
    h1 = _front(x2d, nw1, w_in_t, w_out[0], meta_tokens.astype(x.dtype), gw2, gb, gnw, pw, ps,
                batch, seq, FRONT_TB)
    out = _mlp(h1, nw2, mlp_w1[0], mlp_w2[0], fnw, MLP_TM, MLP_TF)
    return out.reshape(batch, seq, d)
```

```python
import functools

import jax
import jax.numpy as jnp
from jax import lax
from jax.experimental import pallas as pl
from jax.experimental.pallas import tpu as pltpu

F32 = jnp.float32
BF16 = jnp.bfloat16

N_META = 16
CHUNK = 64
PAIR = 2 * CHUNK
GLA_HEADS = 4
GLA_DK = 128
GLA_DV = 256
GLA_KW = GLA_HEADS * GLA_DK
GLA_WIDTH = GLA_HEADS * GLA_DV
GATE_RANK = 16
GATE_TAU = 16.0
POOL_WINDOWS = (2, 4, 8, 16)
POOL_GC = 256
POOL_WIDTH = POOL_GC * len(POOL_WINDOWS)
POOL_HIST = 16
MIX_WIDTH = GLA_WIDTH + POOL_WIDTH
COL_Q, COL_K, COL_V = 0, GLA_KW, 2 * GLA_KW
COL_R = COL_V + GLA_WIDTH
COL_PU = COL_R + GLA_WIDTH
PROJ_WIDTH = COL_PU + POOL_WIDTH
EPS = 1e-6
LANES = 128
VMEM_LIMIT = 56 * 1024 * 1024

FRONT_TB = 256
PREP_ROWS = 512
MLP_TM, MLP_TF = 1024, 512


def _rms_scale(x):
    return lax.rsqrt(jnp.mean(x * x, axis=-1, keepdims=True) + EPS)


def _dot(a, b):
    return jnp.dot(a, b, preferred_element_type=F32)


def _dot_nt(a, b):
    return lax.dot_general(a, b, (((1,), (1,)), ((), ())), preferred_element_type=F32)


def _dot_tn(a, b):
    return lax.dot_general(a, b, (((0,), (0,)), ((), ())), preferred_element_type=F32)


def _gate_logits(glr, gw2_ref, gb_ref):
    return _dot(glr.astype(BF16), gw2_ref[...]) + gb_ref[...]


def _log_sigmoid(g):
    return jnp.minimum(g, 0.0) - jnp.log(1.0 + jnp.exp(-jnp.abs(g)))


def _log_gate(glr, gw2_ref, gb_ref):
    return _log_sigmoid(_gate_logits(glr, gw2_ref, gb_ref)) * (1.0 / GATE_TAU)


def _rows_matmul_f32(sel_bf16, x):
    hi = x.astype(BF16)
    lo = (x - hi.astype(F32)).astype(BF16)
    return _dot(sel_bf16, hi) + _dot(sel_bf16, lo)


def _prepare_weights(wint_hbm, wout_hbm, stage_ref, sem, wm_ref, wg_ref, wo_ref):
    rows = stage_ref.shape[1]
    n_win = PROJ_WIDTH // rows
    n_wout = wout_hbm.shape[0] // rows

    def copy(src_hbm, start, slot):
        return pltpu.make_async_copy(src_hbm.at[pl.ds(start, rows), :], stage_ref.at[slot],
                                     sem.at[slot])

    def win_start(j):
        col = j * rows
        src = jnp.where(j < n_win, col + jnp.where(col >= COL_PU, GATE_RANK, 0), COL_PU)
        return pl.multiple_of(src, GATE_RANK)

    def staged_t(slot):
        return stage_ref[slot].T.astype(BF16)

    copy(wint_hbm, win_start(0), 0).start()

    def win_pair(i, carry):
        copy(wint_hbm, win_start(2 * i + 1), 1).start()
        copy(wint_hbm, 0, 0).wait()
        wm_ref[i, :, 0:rows] = staged_t(0)
        copy(wint_hbm, win_start(2 * i + 2), 0).start()
        copy(wint_hbm, 0, 1).wait()
        wm_ref[i, :, rows:2 * rows] = staged_t(1)
        return carry

    lax.fori_loop(0, n_win // 2, win_pair, 0)
    copy(wout_hbm, 0, 1).start()
    copy(wint_hbm, 0, 0).wait()
    wg_ref[...] = staged_t(0)[:, 0:LANES]

    def wout_pair(i, carry):
        r_even = pl.multiple_of(2 * i * rows, rows)
        r_odd = pl.multiple_of((2 * i + 1) * rows, rows)
        copy(wout_hbm, r_odd, 0).start()
        copy(wout_hbm, 0, 1).wait()
        wo_ref[pl.ds(r_even, rows), :] = stage_ref[1].astype(BF16)

        @pl.when(2 * i + 2 < n_wout)
        def _():
            copy(wout_hbm, pl.multiple_of((2 * i + 2) * rows, rows), 1).start()

        copy(wout_hbm, 0, 0).wait()
        wo_ref[pl.ds(r_odd, rows), :] = stage_ref[0].astype(BF16)
        return carry

    lax.fori_loop(0, n_wout // 2, wout_pair, 0)


def _init_state(mp_ref, mg_ref, gw2_ref, gb_ref, st_ref, hist_ref):
    npad = PAIR - N_META
    logg = jnp.concatenate(
        [jnp.zeros((npad, GLA_KW), F32), _log_gate(mg_ref[...], gw2_ref, gb_ref)], axis=0)
    row = lax.broadcasted_iota(jnp.int32, (PAIR, PAIR), 0)
    col = lax.broadcasted_iota(jnp.int32, (PAIR, PAIR), 1)
    suffix = jnp.where(col > row, 1.0, 0.0).astype(BF16)
    to_end = _rows_matmul_f32(suffix, logg)
    k_m = mp_ref[:, COL_K:COL_K + GLA_KW].astype(F32)
    k_end = (jnp.concatenate([jnp.zeros((npad, GLA_KW), F32), k_m], axis=0)
             * jnp.exp(to_end)).astype(BF16)
    v_m = mp_ref[:, COL_V:COL_V + GLA_WIDTH].astype(F32)
    v_t = jnp.concatenate([jnp.zeros((npad, GLA_WIDTH), F32), v_m], axis=0).T.astype(BF16)
    for h in range(GLA_HEADS):
        st_ref[h] = _dot(v_t[h * GLA_DV:(h + 1) * GLA_DV, :],
                         k_end[:, h * GLA_DK:(h + 1) * GLA_DK])
    hist_ref[...] = mp_ref[:, COL_PU:COL_PU + POOL_WIDTH]


def _proj_norm(x_ref, nw_ref, wg_ref, xn_ref, g_ref):
    x = x_ref[...]
    xn = (x * _rms_scale(x) * nw_ref[...]).astype(BF16)
    xn_ref[...] = xn
    g_ref[...] = _dot(xn, wg_ref[...])


def _proj_chunk(c, xn_ref, wm_ref, p_ref):
    cw = wm_ref.shape[2]
    p_ref[:, c * cw:(c + 1) * cw] = _dot(xn_ref[...], wm_ref[c]).astype(p_ref.dtype)


def _mix_block(p_ref, g_ref, gw2_ref, gb_ref, gnw_ref, pw_ref, ps_ref, bandc_ref, bandh_ref,
               st_ref, hist_ref, o_ref, fillers):
    tb = p_ref.shape[0]
    fillers = list(fillers)

    def fill():
        for emit in (fillers.pop(0) if fillers else ()):
            emit()

    g_raw = _gate_logits(g_ref[...], gw2_ref, gb_ref)
    fill()
    logg = _log_sigmoid(g_raw) * (1.0 / GATE_TAU)
    row = lax.broadcasted_iota(jnp.int32, (tb, tb), 0)
    col = lax.broadcasted_iota(jnp.int32, (tb, tb), 1)
    same_chunk = (row // CHUNK) == (col // CHUNK)
    cum_sel = jnp.where(same_chunk & (col <= row), 1.0, 0.0).astype(BF16)
    g_cum = _rows_matmul_f32(cum_sel, logg)
    causal = (same_chunk & (col <= row))[0:PAIR, 0:PAIR]
    fill()

    n_pairs = tb // PAIR
    units = [(p, h) for p in range(n_pairs) for h in range(GLA_HEADS)]
    rows = lambda p, c: slice(p * PAIR + c * CHUNK, p * PAIR + (c + 1) * CHUNK)
    slab = lambda p: slice(p * PAIR, (p + 1) * PAIR)
    ck = lambda h: slice(h * GLA_DK, (h + 1) * GLA_DK)
    cv = lambda h: slice(COL_V + h * GLA_DV, COL_V + (h + 1) * GLA_DV)
    cr = lambda h: slice(COL_R + h * GLA_DV, COL_R + (h + 1) * GLA_DV)

    q_dec, k_inv, k_end, dec = [], [], [], []
    for p in range(n_pairs):
        gc = g_cum[slab(p)]
        gl = [gc[CHUNK - 1:CHUNK], gc[PAIR - 1:PAIR]]
        gt = jnp.concatenate([jnp.broadcast_to(g, (CHUNK, GLA_KW)) for g in gl], axis=0)
        qf = p_ref[slab(p), COL_Q:COL_Q + GLA_KW].astype(F32)
        kf = p_ref[slab(p), COL_K:COL_K + GLA_KW].astype(F32)
        q_dec.append((qf * (GLA_DK ** -0.5) * jnp.exp(gc)).astype(BF16))
        k_inv.append((kf * jnp.exp(-gc)).astype(BF16))
        k_end.append((kf * jnp.exp(gt - gc)).astype(BF16))
        dec.append([jnp.exp(g) for g in gl])

    fill()
    scores = {u: _dot_nt(q_dec[u[0]][:, ck(u[1])], k_inv[u[0]][:, ck(u[1])]) for u in units}
    d_st = {(p, h, c): _dot_tn(p_ref[rows(p, c), cv(h)],
                               k_end[p][c * CHUNK:(c + 1) * CHUNK, ck(h)])
            for (p, h) in units for c in range(2)}

    fill()
    scores = {u: jnp.where(causal, scores[u], 0.0).astype(BF16) for u in units}
    st_in = {}
    for h in range(GLA_HEADS):
        st = st_ref[h]
        for p in range(n_pairs):
            for c in range(2):
                st_in[(p, h, c)] = st.astype(BF16)
                st = st * dec[p][c][:, ck(h)] + d_st[(p, h, c)]
        st_ref[h] = st

    fill()
    outs = {}
    for (p, h) in units:
        o = _dot(scores[(p, h)], p_ref[slab(p), cv(h)])
        outs[(p, h)] = o + jnp.concatenate(
            [_dot_nt(q_dec[p][c * CHUNK:(c + 1) * CHUNK, ck(h)], st_in[(p, h, c)])
             for c in range(2)], axis=0)

    n_groups = len(POOL_WINDOWS)
    cs = lambda g: slice(g * POOL_GC, (g + 1) * POOL_GC)
    cp = lambda g: slice(COL_PU + g * POOL_GC, COL_PU + (g + 1) * POOL_GC)
    y_cur = [_dot(bandc_ref[g], p_ref[:, cp(g)]) for g in range(n_groups)]
    y_hist = [_dot(bandh_ref[g], hist_ref[:, cs(g)]) for g in range(n_groups)]
    hist_ref[...] = p_ref[tb - POOL_HIST:tb, COL_PU:COL_PU + POOL_WIDTH]
    y = [jnp.concatenate([y_cur[g][0:POOL_HIST] + y_hist[g], y_cur[g][POOL_HIST:]],
                         axis=0).astype(BF16) for g in range(n_groups)]
    y = [_dot(y[g], pw_ref[g]) for g in range(n_groups)]

    while fillers:
        fill()
    gnw = gnw_ref[...]
    for (p, h) in units:
        o = outs[(p, h)]
        o = o * _rms_scale(o) * gnw
        rf = p_ref[slab(p), cr(h)].astype(F32)
        gate = rf * (1.0 / (1.0 + jnp.exp(-rf)))
        o_ref[slab(p), h * GLA_DV:(h + 1) * GLA_DV] = (o * gate).astype(o_ref.dtype)
    for g in range(n_groups):
        o_ref[:, GLA_WIDTH + g * POOL_GC:GLA_WIDTH + (g + 1) * POOL_GC] = (
            y[g] * ps_ref[:, cs(g)]).astype(o_ref.dtype)


def _front_kernel(nt, x_ref, xnext_ref, nw_ref, wint_hbm, wout_hbm, meta_ref,
                  gw2_ref, gb_ref, gnw_ref, pw_ref, ps_ref, bandc_ref, bandh_ref,
                  h_ref,
                  wm_ref, wg_ref, wo_ref, stage_ref, sem, mp_ref, mg_ref, st_ref, hist_ref,
                  xn_ref, p_ref, g_ref, m_ref):
    s = pl.program_id(0)

    @pl.when(s == 0)
    def _():
        _prepare_weights(wint_hbm, wout_hbm, stage_ref, sem, wm_ref, wg_ref, wo_ref)
        _proj_norm(meta_ref, nw_ref, wg_ref, xn_ref.at[pl.ds(0, N_META), :], mg_ref)
        for c in range(wm_ref.shape[0]):
            _proj_chunk(c, xn_ref.at[pl.ds(0, N_META), :], wm_ref, mp_ref)
        _proj_norm(x_ref, nw_ref, wg_ref, xn_ref, g_ref)

    @pl.when(lax.rem(s, nt) == 0)
    def _():
        _init_state(mp_ref, mg_ref, gw2_ref, gb_ref, st_ref, hist_ref)

    chunk = lambda c: functools.partial(_proj_chunk, c, xn_ref, wm_ref, p_ref)
    chunks = lambda lo, hi: [chunk(c) for c in range(lo // wm_ref.shape[2], hi // wm_ref.shape[2])]
    pool_mid = COL_PU + POOL_WIDTH // 2
    _mix_block(p_ref, g_ref, gw2_ref, gb_ref, gnw_ref, pw_ref, ps_ref, bandc_ref, bandh_ref,
               st_ref, hist_ref, m_ref,
               [chunks(COL_Q, COL_K), chunks(COL_K, COL_V), chunks(COL_V, COL_R),
                chunks(COL_PU, pool_mid), chunks(pool_mid, PROJ_WIDTH), chunks(COL_R, COL_PU)])
    h_ref[...] = x_ref[...] + _dot(m_ref[...], wo_ref[...])
    _proj_norm(xnext_ref, nw_ref, wg_ref, xn_ref, g_ref)


def _pool_bands(tb):
    t = jnp.arange(tb)[:, None] + POOL_HIST
    j = jnp.arange(tb + POOL_HIST)[None, :]
    mats = [jnp.where((j <= t) & (j > t - w), 1.0 / w, 0.0) - jnp.where(j == t, 1.0, 0.0)
            for w in POOL_WINDOWS]
    band = jnp.stack(mats).astype(BF16)
    return band[:, :, POOL_HIST:], band[:, :POOL_HIST, :POOL_HIST]


def _front(x2d, nw, w_in_t, w_out, meta, gw2, gb, gnw, pw, ps, batch, seq, tb):
    nt = seq // tb
    d = x2d.shape[1]
    assert w_in_t.shape == (PROJ_WIDTH + GATE_RANK, d) and PROJ_WIDTH % PREP_ROWS == 0
    assert all(c % PREP_ROWS == 0 for c in (COL_V, COL_R, COL_PU, w_out.shape[0]))
    bandc, bandh = _pool_bands(tb)
    full = lambda a: pl.BlockSpec(a.shape, lambda s: (0,) * a.ndim)
    return pl.pallas_call(
        functools.partial(_front_kernel, nt),
        grid=(batch * nt,),
        in_specs=[
            pl.BlockSpec((tb, d), lambda s: (s, 0)),
            pl.BlockSpec((tb, d), lambda s: (jnp.minimum(s + 1, batch * nt - 1), 0)),
            full(nw),
            pl.BlockSpec(memory_space=pl.ANY),
            pl.BlockSpec(memory_space=pl.ANY),
            full(meta), full(gw2), full(gb), full(gnw), full(pw), full(ps),
            full(bandc), full(bandh),
        ],
        out_specs=pl.BlockSpec((tb, d), lambda s: (s, 0)),
        out_shape=jax.ShapeDtypeStruct((batch * seq, d), F32),
        scratch_shapes=[
            pltpu.VMEM((PROJ_WIDTH // PREP_ROWS, d, PREP_ROWS), BF16),
            pltpu.VMEM((d, LANES), BF16),
            pltpu.VMEM(w_out.shape, BF16),
            pltpu.VMEM((2, PREP_ROWS // 2, d), F32),
            pltpu.SemaphoreType.DMA((2,)),
            pltpu.VMEM((N_META, PROJ_WIDTH), BF16),
            pltpu.VMEM((N_META, LANES), F32),
            pltpu.VMEM((GLA_HEADS, GLA_DV, GLA_DK), F32),
            pltpu.VMEM((POOL_HIST, POOL_WIDTH), BF16),
            pltpu.VMEM((tb, d), BF16),
            pltpu.VMEM((tb, PROJ_WIDTH), BF16),
            pltpu.VMEM((tb, LANES), F32),
            pltpu.VMEM((tb, MIX_WIDTH), BF16),
        ],
        compiler_params=pltpu.CompilerParams(
            dimension_semantics=("arbitrary",), vmem_limit_bytes=VMEM_LIMIT),
        name="front",
    )(x2d, x2d, nw, w_in_t, w_out, meta, gw2, gb, gnw, pw, ps, bandc, bandh)


def _mlp_kernel(h_hbm, nw_ref, w1_ref, w2_ref, fnw_ref, o_ref, xn_ref, hbuf_ref, sem):
    m = pl.program_id(0)
    f = pl.program_id(1)
    tm = hbuf_ref.shape[0]

    def fetch(tile):
        return pltpu.make_async_copy(h_hbm.at[pl.ds(tile * tm, tm), :], hbuf_ref, sem)

    @pl.when((m == 0) & (f == 0))
    def _():
        fetch(0).start()

    @pl.when(f == 0)
    def _():
        fetch(m).wait()
        h = hbuf_ref[...]
        xn_ref[...] = (h * _rms_scale(h) * nw_ref[...]).astype(BF16)
        o_ref[...] = h

    @pl.when((f == pl.num_programs(1) // 2) & (m + 1 < pl.num_programs(0)))
    def _():
        fetch(m + 1).start()

    z = _dot(xn_ref[...], w1_ref[...].astype(BF16))
    a = jnp.square(jnp.maximum(z, 0.0)).astype(BF16)
    o_ref[...] += _dot(a, w2_ref[...].astype(BF16))

    @pl.when(f == pl.num_programs(1) - 1)
    def _():
        y = o_ref[...]
        o_ref[...] = y * _rms_scale(y) * fnw_ref[...]


def _mlp(h, nw, w1, w2, fnw, tm, tf):
    m, d = h.shape
    ff = w1.shape[1]
    return pl.pallas_call(
        _mlp_kernel,
        grid=(m // tm, ff // tf),
        in_specs=[
            pl.BlockSpec(memory_space=pl.ANY),
            pl.BlockSpec((1, d), lambda i, j: (0, 0)),
            pl.BlockSpec((d, tf), lambda i, j: (0, j)),
            pl.BlockSpec((tf, d), lambda i, j: (j, 0)),
            pl.BlockSpec((1, d), lambda i, j: (0, 0)),
        ],
        out_specs=pl.BlockSpec((tm, d), lambda i, j: (i, 0)),
        out_shape=jax.ShapeDtypeStruct((m, d), F32),
        scratch_shapes=[
            pltpu.VMEM((tm, d), BF16),
            pltpu.VMEM((tm, d), F32),
            pltpu.SemaphoreType.DMA(()),
        ],
        compiler_params=pltpu.CompilerParams(
            dimension_semantics=("arbitrary", "arbitrary"),
            vmem_limit_bytes=VMEM_LIMIT),
        name="mlp",
    )(h, nw, w1, w2, fnw)


def kernel(x, meta_tokens, norm1_w, w_in, gate_w2, gate_b, gla_norm_w, pool_w, pool_scale,
           w_out, norm2_w, mlp_w1, mlp_w2, final_norm_w):
    batch, seq, d = x.shape
    assert w_in.shape[0] == 1, "single-layer trunk"
    assert meta_tokens.shape[0] == N_META

    w_in_t = w_in[0].T
    gw2 = jnp.pad(gate_w2[0], ((0, LANES - GATE_RANK), (0, 0))).astype(BF16)
    gb = gate_b[0].reshape(1, GLA_KW)
    gnw = gla_norm_w[0].reshape(1, GLA_DV)
    pw = pool_w[0].astype(BF16)
    ps = pool_scale[0].reshape(1, POOL_WIDTH)
    nw1 = norm1_w[0].reshape(1, d)
    nw2 = norm2_w[0].reshape(1, d)
    fnw = final_norm_w.reshape(1, d)

    x2d = x.reshape(batch * seq, d)
    h1 = _front(x2d, nw1, w_in_t, w_out[0], meta_tokens.astype(x.dtype), gw2, gb, gnw, pw, ps,
                batch, seq, FRONT_TB)
    out = _mlp(h1, nw2, mlp_w1[0], mlp_w2[0], fnw, MLP_TM, MLP_TF)
    return out.reshape(batch, seq, d)
```

```python
import functools

import jax
import jax.numpy as jnp
from jax import lax
from jax.experimental import pallas as pl
from jax.experimental.pallas import tpu as pltpu

F32 = jnp.float32
BF16 = jnp.bfloat16

N_META = 16
CHUNK = 64
PAIR = 2 * CHUNK
GLA_HEADS = 4
GLA_DK = 128
GLA_DV = 256
GLA_KW = GLA_HEADS * GLA_DK
GLA_WIDTH = GLA_HEADS * GLA_DV
GATE_RANK = 16
GATE_TAU = 16.0
POOL_WINDOWS = (2, 4, 8, 16)
POOL_GC = 256
POOL_WIDTH = POOL_GC * len(POOL_WINDOWS)
POOL_HIST = 16
MIX_WIDTH = GLA_WIDTH + POOL_WIDTH
COL_Q, COL_K, COL_V = 0, GLA_KW, 2 * GLA_KW
COL_R = COL_V + GLA_WIDTH
COL_PU = COL_R + GLA_WIDTH
PROJ_WIDTH = COL_PU + POOL_WIDTH
EPS = 1e-6
LANES = 128
VMEM_LIMIT = 56 * 1024 * 1024
FRONT_VMEM_LIMIT = 61 * 1024 * 1024

FRONT_TB = 256
PREP_ROWS = 512
MLP_TM, MLP_TF = 1024, 1024


def _rms_scale(x):
    return lax.rsqrt(jnp.mean(x * x, axis=-1, keepdims=True) + EPS)


def _dot(a, b):
    return jnp.dot(a, b, preferred_element_type=F32)


def _dot_nt(a, b):
    return lax.dot_general(a, b, (((1,), (1,)), ((), ())), preferred_element_type=F32)


def _dot_tn(a, b):
    return lax.dot_general(a, b, (((0,), (0,)), ((), ())), preferred_element_type=F32)


def _gate_logits(glr, gw2_ref, gb_ref):
    return _dot(glr.astype(BF16), gw2_ref[...]) + gb_ref[...]


def _log_sigmoid(g):
    return jnp.minimum(g, 0.0) - jnp.log(1.0 + jnp.exp(-jnp.abs(g)))


def _log_gate(glr, gw2_ref, gb_ref):
    return _log_sigmoid(_gate_logits(glr, gw2_ref, gb_ref)) * (1.0 / GATE_TAU)


def _rows_matmul_f32(sel_bf16, x):
    hi = x.astype(BF16)
    lo = (x - hi.astype(F32)).astype(BF16)
    return _dot(sel_bf16, hi) + _dot(sel_bf16, lo)


def _prepare_weights(wint_hbm, wout_hbm, stage_ref, sem, wm_ref, wg_ref, wo_ref):
    rows = stage_ref.shape[1]
    n_win = PROJ_WIDTH // rows
    n_wout = wout_hbm.shape[0] // rows

    def copy(src_hbm, start, slot):
        return pltpu.make_async_copy(src_hbm.at[pl.ds(start, rows), :], stage_ref.at[slot],
                                     sem.at[slot])

    def win_start(j):
        col = j * rows
        src = jnp.where(j < n_win, col + jnp.where(col >= COL_PU, GATE_RANK, 0), COL_PU)
        return pl.multiple_of(src, GATE_RANK)

    def staged_t(slot):
        return stage_ref[slot].T.astype(BF16)

    copy(wint_hbm, win_start(0), 0).start()

    def win_pair(i, carry):
        copy(wint_hbm, win_start(2 * i + 1), 1).start()
        copy(wint_hbm, 0, 0).wait()
        wm_ref[i, :, 0:rows] = staged_t(0)
        copy(wint_hbm, win_start(2 * i + 2), 0).start()
        copy(wint_hbm, 0, 1).wait()
        wm_ref[i, :, rows:2 * rows] = staged_t(1)
        return carry

    lax.fori_loop(0, n_win // 2, win_pair, 0)
    copy(wout_hbm, 0, 1).start()
    copy(wint_hbm, 0, 0).wait()
    wg_ref[...] = staged_t(0)[:, 0:LANES]

    def wout_pair(i, carry):
        r_even = pl.multiple_of(2 * i * rows, rows)
        r_odd = pl.multiple_of((2 * i + 1) * rows, rows)
        copy(wout_hbm, r_odd, 0).start()
        copy(wout_hbm, 0, 1).wait()
        wo_ref[pl.ds(r_even, rows), :] = stage_ref[1].astype(BF16)

        @pl.when(2 * i + 2 < n_wout)
        def _():
            copy(wout_hbm, pl.multiple_of((2 * i + 2) * rows, rows), 1).start()

        copy(wout_hbm, 0, 0).wait()
        wo_ref[pl.ds(r_odd, rows), :] = stage_ref[0].astype(BF16)
        return carry

    lax.fori_loop(0, n_wout // 2, wout_pair, 0)


def _init_state(mp_ref, mg_ref, gw2_ref, gb_ref, st_ref, hist_ref):
    npad = PAIR - N_META
    logg = jnp.concatenate(
        [jnp.zeros((npad, GLA_KW), F32), _log_gate(mg_ref[...], gw2_ref, gb_ref)], axis=0)
    row = lax.broadcasted_iota(jnp.int32, (PAIR, PAIR), 0)
    col = lax.broadcasted_iota(jnp.int32, (PAIR, PAIR), 1)
    suffix = jnp.where(col > row, 1.0, 0.0).astype(BF16)
    to_end = _rows_matmul_f32(suffix, logg)
    k_m = mp_ref[:, COL_K:COL_K + GLA_KW].astype(F32)
    k_end = (jnp.concatenate([jnp.zeros((npad, GLA_KW), F32), k_m], axis=0)
             * jnp.exp(to_end)).astype(BF16)
    v_m = mp_ref[:, COL_V:COL_V + GLA_WIDTH].astype(F32)
    v_t = jnp.concatenate([jnp.zeros((npad, GLA_WIDTH), F32), v_m], axis=0).T.astype(BF16)
    for h in range(GLA_HEADS):
        st_ref[h] = _dot(v_t[h * GLA_DV:(h + 1) * GLA_DV, :],
                         k_end[:, h * GLA_DK:(h + 1) * GLA_DK])
    hist_ref[...] = mp_ref[:, COL_PU:COL_PU + POOL_WIDTH]


def _proj_norm(x_ref, nw_ref, wg_ref, xn_ref, g_ref):
    x = x_ref[...]
    xn = (x * _rms_scale(x) * nw_ref[...]).astype(BF16)
    xn_ref[...] = xn
    g_ref[...] = _dot(xn, wg_ref[...])


def _proj_chunk(c, xn_ref, wm_ref, p_ref):
    cw = wm_ref.shape[2]
    p_ref[:, c * cw:(c + 1) * cw] = _dot(xn_ref[...], wm_ref[c]).astype(p_ref.dtype)


def _mix_block(p_ref, g_ref, gw2_ref, gb_ref, gnw_ref, pw_ref, ps_ref, bandc_ref, bandh_ref,
               st_ref, hist_ref, o_ref, fillers):
    tb = p_ref.shape[0]
    fillers = list(fillers)

    def fill():
        for emit in (fillers.pop(0) if fillers else ()):
            emit()

    g_raw = _gate_logits(g_ref[...], gw2_ref, gb_ref)
    fill()
    logg = _log_sigmoid(g_raw) * (1.0 / GATE_TAU)
    row = lax.broadcasted_iota(jnp.int32, (tb, tb), 0)
    col = lax.broadcasted_iota(jnp.int32, (tb, tb), 1)
    same_chunk = (row // CHUNK) == (col // CHUNK)
    cum_sel = jnp.where(same_chunk & (col <= row), 1.0, 0.0).astype(BF16)
    g_cum = _rows_matmul_f32(cum_sel, logg)
    causal = (same_chunk & (col <= row))[0:PAIR, 0:PAIR]
    fill()

    n_pairs = tb // PAIR
    units = [(p, h) for p in range(n_pairs) for h in range(GLA_HEADS)]
    rows = lambda p, c: slice(p * PAIR + c * CHUNK, p * PAIR + (c + 1) * CHUNK)
    slab = lambda p: slice(p * PAIR, (p + 1) * PAIR)
    ck = lambda h: slice(h * GLA_DK, (h + 1) * GLA_DK)
    cv = lambda h: slice(COL_V + h * GLA_DV, COL_V + (h + 1) * GLA_DV)
    cr = lambda h: slice(COL_R + h * GLA_DV, COL_R + (h + 1) * GLA_DV)

    q_dec, k_inv, k_end, dec = [], [], [], []
    for p in range(n_pairs):
        gc = g_cum[slab(p)]
        gl = [gc[CHUNK - 1:CHUNK], gc[PAIR - 1:PAIR]]
        gt = jnp.concatenate([jnp.broadcast_to(g, (CHUNK, GLA_KW)) for g in gl], axis=0)
        qf = p_ref[slab(p), COL_Q:COL_Q + GLA_KW].astype(F32)
        kf = p_ref[slab(p), COL_K:COL_K + GLA_KW].astype(F32)
        q_dec.append((qf * (GLA_DK ** -0.5) * jnp.exp(gc)).astype(BF16))
        k_inv.append((kf * jnp.exp(-gc)).astype(BF16))
        k_end.append((kf * jnp.exp(gt - gc)).astype(BF16))
        dec.append([jnp.exp(g) for g in gl])

    fill()
    scores = {u: _dot_nt(q_dec[u[0]][:, ck(u[1])], k_inv[u[0]][:, ck(u[1])]) for u in units}
    d_st = {(p, h, c): _dot_tn(p_ref[rows(p, c), cv(h)],
                               k_end[p][c * CHUNK:(c + 1) * CHUNK, ck(h)])
            for (p, h) in units for c in range(2)}

    fill()
    scores = {u: jnp.where(causal, scores[u], 0.0).astype(BF16) for u in units}
    st_in = {}
    for h in range(GLA_HEADS):
        st = st_ref[h]
        for p in range(n_pairs):
            for c in range(2):
                st_in[(p, h, c)] = st.astype(BF16)
                st = st * dec[p][c][:, ck(h)] + d_st[(p, h, c)]
        st_ref[h] = st

    fill()
    outs = {}
    for (p, h) in units:
        o = _dot(scores[(p, h)], p_ref[slab(p), cv(h)])
        outs[(p, h)] = o + jnp.concatenate(
            [_dot_nt(q_dec[p][c * CHUNK:(c + 1) * CHUNK, ck(h)], st_in[(p, h, c)])
             for c in range(2)], axis=0)

    n_groups = len(POOL_WINDOWS)
    cs = lambda g: slice(g * POOL_GC, (g + 1) * POOL_GC)
    cp = lambda g: slice(COL_PU + g * POOL_GC, COL_PU + (g + 1) * POOL_GC)
    y_cur = [_dot(bandc_ref[g], p_ref[:, cp(g)]) for g in range(n_groups)]
    y_hist = [_dot(bandh_ref[g], hist_ref[:, cs(g)]) for g in range(n_groups)]
    hist_ref[...] = p_ref[tb - POOL_HIST:tb, COL_PU:COL_PU + POOL_WIDTH]
    y = [jnp.concatenate([y_cur[g][0:POOL_HIST] + y_hist[g], y_cur[g][POOL_HIST:]],
                         axis=0).astype(BF16) for g in range(n_groups)]
    y = [_dot(y[g], pw_ref[g]) for g in range(n_groups)]

    while fillers:
        fill()
    gnw = gnw_ref[...]
    for (p, h) in units:
        o = outs[(p, h)]
        o = o * _rms_scale(o) * gnw
        rf = p_ref[slab(p), cr(h)].astype(F32)
        gate = rf * (1.0 / (1.0 + jnp.exp(-rf)))
        o_ref[slab(p), h * GLA_DV:(h + 1) * GLA_DV] = (o * gate).astype(o_ref.dtype)
    for g in range(n_groups):
        o_ref[:, GLA_WIDTH + g * POOL_GC:GLA_WIDTH + (g + 1) * POOL_GC] = (
            y[g] * ps_ref[:, cs(g)]).astype(o_ref.dtype)


def _front_kernel(nt, x_ref, xnext_ref, nw_ref, wint_hbm, wout_hbm, meta_ref,
                  gw2_ref, gb_ref, gnw_ref, pw_ref, ps_ref, bandc_ref, bandh_ref,
                  w1_ref, w2_ref,
                  h_ref, w1b_ref, w2b_ref,
                  wm_ref, wg_ref, wo_ref, stage_ref, sem, mp_ref, mg_ref, st_ref, hist_ref,
                  xn_ref, p_ref, g_ref, m_ref):
    s = pl.program_id(0)

    @pl.when(s == 0)
    def _():
        _prepare_weights(wint_hbm, wout_hbm, stage_ref, sem, wm_ref, wg_ref, wo_ref)
        _proj_norm(meta_ref, nw_ref, wg_ref, xn_ref.at[pl.ds(0, N_META), :], mg_ref)
        for c in range(wm_ref.shape[0]):
            _proj_chunk(c, xn_ref.at[pl.ds(0, N_META), :], wm_ref, mp_ref)
        _proj_norm(x_ref, nw_ref, wg_ref, xn_ref, g_ref)

    @pl.when(lax.rem(s, nt) == 0)
    def _():
        _init_state(mp_ref, mg_ref, gw2_ref, gb_ref, st_ref, hist_ref)

    chunk = lambda c: functools.partial(_proj_chunk, c, xn_ref, wm_ref, p_ref)
    chunks = lambda lo, hi: [chunk(c) for c in range(lo // wm_ref.shape[2], hi // wm_ref.shape[2])]
    pool_mid = COL_PU + POOL_WIDTH // 2
    _mix_block(p_ref, g_ref, gw2_ref, gb_ref, gnw_ref, pw_ref, ps_ref, bandc_ref, bandh_ref,
               st_ref, hist_ref, m_ref,
               [chunks(COL_Q, COL_K), chunks(COL_K, COL_V), chunks(COL_V, COL_R),
                chunks(COL_PU, pool_mid), chunks(pool_mid, PROJ_WIDTH), chunks(COL_R, COL_PU)])
    w1b_ref[...] = w1_ref[...].astype(BF16)
    w2b_ref[...] = w2_ref[...].astype(BF16)
    h_ref[...] = x_ref[...] + _dot(m_ref[...], wo_ref[...])
    _proj_norm(xnext_ref, nw_ref, wg_ref, xn_ref, g_ref)


def _pool_bands(tb):
    t = jnp.arange(tb)[:, None] + POOL_HIST
    j = jnp.arange(tb + POOL_HIST)[None, :]
    mats = [jnp.where((j <= t) & (j > t - w), 1.0 / w, 0.0) - jnp.where(j == t, 1.0, 0.0)
            for w in POOL_WINDOWS]
    band = jnp.stack(mats).astype(BF16)
    return band[:, :, POOL_HIST:], band[:, :POOL_HIST, :POOL_HIST]


def _front(x2d, nw, w_in_t, w_out, meta, gw2, gb, gnw, pw, ps, w1, w2, batch, seq, tb):
    nt = seq // tb
    n_steps = batch * nt
    d = x2d.shape[1]
    assert w1.shape[0] % (8 * n_steps) == 0 and w2.shape[0] % (8 * n_steps) == 0
    assert w_in_t.shape == (PROJ_WIDTH + GATE_RANK, d) and PROJ_WIDTH % PREP_ROWS == 0
    assert all(c % PREP_ROWS == 0 for c in (COL_V, COL_R, COL_PU, w_out.shape[0]))
    bandc, bandh = _pool_bands(tb)
    full = lambda a: pl.BlockSpec(a.shape, lambda s: (0,) * a.ndim)
    return pl.pallas_call(
        functools.partial(_front_kernel, nt),
        grid=(batch * nt,),
        in_specs=[
            pl.BlockSpec((tb, d), lambda s: (s, 0)),
            pl.BlockSpec((tb, d), lambda s: (jnp.minimum(s + 1, batch * nt - 1), 0)),
            full(nw),
            pl.BlockSpec(memory_space=pl.ANY),
            pl.BlockSpec(memory_space=pl.ANY),
            full(meta), full(gw2), full(gb), full(gnw), full(pw), full(ps),
            full(bandc), full(bandh),
            pl.BlockSpec((w1.shape[0] // n_steps, w1.shape[1]), lambda s: (s, 0)),
            pl.BlockSpec((w2.shape[0] // n_steps, w2.shape[1]), lambda s: (s, 0)),
        ],
        out_specs=[
            pl.BlockSpec((tb, d), lambda s: (s, 0)),
            pl.BlockSpec((w1.shape[0] // n_steps, w1.shape[1]), lambda s: (s, 0)),
            pl.BlockSpec((w2.shape[0] // n_steps, w2.shape[1]), lambda s: (s, 0)),
        ],
        out_shape=[
            jax.ShapeDtypeStruct((batch * seq, d), F32),
            jax.ShapeDtypeStruct(w1.shape, BF16),
            jax.ShapeDtypeStruct(w2.shape, BF16),
        ],
        scratch_shapes=[
            pltpu.VMEM((PROJ_WIDTH // PREP_ROWS, d, PREP_ROWS), BF16),
            pltpu.VMEM((d, LANES), BF16),
            pltpu.VMEM(w_out.shape, BF16),
            pltpu.VMEM((2, PREP_ROWS // 2, d), F32),
            pltpu.SemaphoreType.DMA((2,)),
            pltpu.VMEM((N_META, PROJ_WIDTH), BF16),
            pltpu.VMEM((N_META, LANES), F32),
            pltpu.VMEM((GLA_HEADS, GLA_DV, GLA_DK), F32),
            pltpu.VMEM((POOL_HIST, POOL_WIDTH), BF16),
            pltpu.VMEM((tb, d), BF16),
            pltpu.VMEM((tb, PROJ_WIDTH), BF16),
            pltpu.VMEM((tb, LANES), F32),
            pltpu.VMEM((tb, MIX_WIDTH), BF16),
        ],
        compiler_params=pltpu.CompilerParams(
            dimension_semantics=("arbitrary",), vmem_limit_bytes=FRONT_VMEM_LIMIT),
        name="front",
    )(x2d, x2d, nw, w_in_t, w_out, meta, gw2, gb, gnw, pw, ps, bandc, bandh, w1, w2)


def _mlp_kernel(h_hbm, nw_ref, w1_ref, w2_ref, fnw_ref, o_ref, xn_ref, hbuf_ref, sem):
    m = pl.program_id(0)
    f = pl.program_id(1)
    tm = hbuf_ref.shape[0]

    def fetch(tile):
        return pltpu.make_async_copy(h_hbm.at[pl.ds(tile * tm, tm), :], hbuf_ref, sem)

    @pl.when((m == 0) & (f == 0))
    def _():
        fetch(0).start()

    @pl.when(f == 0)
    def _():
        fetch(m).wait()
        h = hbuf_ref[...]
        xn_ref[...] = (h * _rms_scale(h) * nw_ref[...]).astype(BF16)
        o_ref[...] = h

    @pl.when((f == pl.num_programs(1) // 2) & (m + 1 < pl.num_programs(0)))
    def _():
        fetch(m + 1).start()

    z = _dot(xn_ref[...], w1_ref[...])
    a = jnp.square(jnp.maximum(z, 0.0)).astype(BF16)
    o_ref[...] += _dot(a, w2_ref[...])

    @pl.when(f == pl.num_programs(1) - 1)
    def _():
        y = o_ref[...]
        o_ref[...] = y * _rms_scale(y) * fnw_ref[...]


def _mlp(h, nw, w1, w2, fnw, tm, tf):
    m, d = h.shape
    ff = w1.shape[1]
    return pl.pallas_call(
        _mlp_kernel,
        grid=(m // tm, ff // tf),
        in_specs=[
            pl.BlockSpec(memory_space=pl.ANY),
            pl.BlockSpec((1, d), lambda i, j: (0, 0)),
            pl.BlockSpec((d, tf), lambda i, j: (0, j)),
            pl.BlockSpec((tf, d), lambda i, j: (j, 0)),
            pl.BlockSpec((1, d), lambda i, j: (0, 0)),
        ],
        out_specs=pl.BlockSpec((tm, d), lambda i, j: (i, 0)),
        out_shape=jax.ShapeDtypeStruct((m, d), F32),
        scratch_shapes=[
            pltpu.VMEM((tm, d), BF16),
            pltpu.VMEM((tm, d), F32),
            pltpu.SemaphoreType.DMA(()),
        ],
        compiler_params=pltpu.CompilerParams(
            dimension_semantics=("arbitrary", "arbitrary"),
            vmem_limit_bytes=VMEM_LIMIT),
        name="mlp",
    )(h, nw, w1, w2, fnw)


def kernel(x, meta_tokens, norm1_w, w_in, gate_w2, gate_b, gla_norm_w, pool_w, pool_scale,
           w_out, norm2_w, mlp_w1, mlp_w2, final_norm_w):
    batch, seq, d = x.shape
    assert w_in.shape[0] == 1, "single-layer trunk"
    assert meta_tokens.shape[0] == N_META

    w_in_t = w_in[0].T
    gw2 = jnp.pad(gate_w2[0], ((0, LANES - GATE_RANK), (0, 0))).astype(BF16)
    gb = gate_b[0].reshape(1, GLA_KW)
    gnw = gla_norm_w[0].reshape(1, GLA_DV)
    pw = pool_w[0].astype(BF16)
    ps = pool_scale[0].reshape(1, POOL_WIDTH)
    nw1 = norm1_w[0].reshape(1, d)
    nw2 = norm2_w[0].reshape(1, d)
    fnw = final_norm_w.reshape(1, d)

    x2d = x.reshape(batch * seq, d)
    h1, w1b, w2b = _front(x2d, nw1, w_in_t, w_out[0], meta_tokens.astype(x.dtype), gw2, gb, gnw,
                          pw, ps, mlp_w1[0], mlp_w2[0], batch, seq, FRONT_TB)
    out = _mlp(h1, nw2, w1b, w2b, fnw, MLP_TM, MLP_TF)
    return out.reshape(batch, seq, d)
```

```python
import functools

import jax
import jax.numpy as jnp
from jax import lax
from jax.experimental import pallas as pl
from jax.experimental.pallas import tpu as pltpu

F32 = jnp.float32
BF16 = jnp.bfloat16

N_META = 16
CHUNK = 64
PAIR = 2 * CHUNK
GLA_HEADS = 4
GLA_DK = 128
GLA_DV = 256
GLA_KW = GLA_HEADS * GLA_DK
GLA_WIDTH = GLA_HEADS * GLA_DV
GATE_RANK = 16
GATE_TAU = 16.0
POOL_WINDOWS = (2, 4, 8, 16)
POOL_GC = 256
POOL_WIDTH = POOL_GC * len(POOL_WINDOWS)
POOL_HIST = 16
MIX_WIDTH = GLA_WIDTH + POOL_WIDTH
COL_Q, COL_K, COL_V = 0, GLA_KW, 2 * GLA_KW
COL_R = COL_V + GLA_WIDTH
COL_PU = COL_R + GLA_WIDTH
PROJ_WIDTH = COL_PU + POOL_WIDTH
EPS = 1e-6
LANES = 128
VMEM_CAPACITY = 64 * 1024 * 1024
VMEM_LIMIT = VMEM_CAPACITY - 8 * 1024 * 1024
FRONT_VMEM_LIMIT = VMEM_CAPACITY - 3 * 1024 * 1024

FRONT_TB = 256
PREP_ROWS = 512
MLP_TM, MLP_TF = 1024, 1024


def _rms_scale(x):
    return lax.rsqrt(jnp.mean(x * x, axis=-1, keepdims=True) + EPS)


def _dot(a, b):
    return jnp.dot(a, b, preferred_element_type=F32)


def _dot_nt(a, b):
    return lax.dot_general(a, b, (((1,), (1,)), ((), ())), preferred_element_type=F32)


def _dot_tn(a, b):
    return lax.dot_general(a, b, (((0,), (0,)), ((), ())), preferred_element_type=F32)


def _gate_logits(glr, gw2_ref, gb_ref):
    return _dot(glr.astype(BF16), gw2_ref[...]) + gb_ref[...]


def _log_sigmoid(g):
    return jnp.minimum(g, 0.0) - jnp.log(1.0 + jnp.exp(-jnp.abs(g)))


def _log_gate(glr, gw2_ref, gb_ref):
    return _log_sigmoid(_gate_logits(glr, gw2_ref, gb_ref)) * (1.0 / GATE_TAU)


def _chunk_cumsum(x):
    pos = lax.broadcasted_iota(jnp.int32, x.shape, 0) % CHUNK
    k = 1
    while k < CHUNK:
        x = x + jnp.where(pos >= k, pltpu.roll(x, k, axis=0), 0.0)
        k *= 2
    return x


def _rows_matmul_f32(sel_bf16, x):
    hi = x.astype(BF16)
    lo = (x - hi.astype(F32)).astype(BF16)
    return _dot(sel_bf16, hi) + _dot(sel_bf16, lo)


def _prepare_weights(wint_hbm, wout_hbm, stage_ref, sem, wm_ref, wg_ref, wo_ref):
    n_slots, rows = stage_ref.shape[0], stage_ref.shape[1]
    n_win_groups = wm_ref.shape[0]
    n_wout_groups = wout_hbm.shape[0] // (rows * n_slots)
    assert wm_ref.shape[2] == n_slots * rows and rows == LANES

    def copy(src_hbm, start, slot):
        return pltpu.make_async_copy(src_hbm.at[pl.ds(start, rows), :], stage_ref.at[slot],
                                     sem.at[slot])

    def win_start(piece):
        col = piece * rows
        return pl.multiple_of(col + jnp.where(col >= COL_PU, GATE_RANK, 0), GATE_RANK)

    def wout_start(piece):
        return pl.multiple_of(piece * rows, rows)

    for j in range(n_slots):
        copy(wint_hbm, win_start(j), j).start()

    def win_group(i, carry):
        for j in range(n_slots):
            copy(wint_hbm, 0, j).wait()
            wm_ref[i, :, j * rows:(j + 1) * rows] = stage_ref[j].T.astype(BF16)

            @pl.when(i + 1 < n_win_groups)
            def _():
                copy(wint_hbm, win_start(n_slots * (i + 1) + j), j).start()

            @pl.when(i + 1 == n_win_groups)
            def _():
                if j == 0:
                    copy(wint_hbm, COL_PU, 0).start()
                else:
                    copy(wout_hbm, wout_start(j - 1), j).start()
        return carry

    lax.fori_loop(0, n_win_groups, win_group, 0)
    copy(wint_hbm, 0, 0).wait()
    wg_ref[...] = stage_ref[0].T.astype(BF16)
    copy(wout_hbm, wout_start(n_slots - 1), 0).start()

    def wout_group(k, carry):
        for j in range(n_slots):
            slot = (j + 1) % n_slots
            piece = n_slots * k + j
            copy(wout_hbm, 0, slot).wait()
            wo_ref[pl.ds(wout_start(piece), rows), :] = stage_ref[slot].astype(BF16)

            @pl.when(k + 1 < n_wout_groups)
            def _():
                copy(wout_hbm, wout_start(piece + n_slots), slot).start()
        return carry

    lax.fori_loop(0, n_wout_groups, wout_group, 0)


def _init_state(mp_ref, mg_ref, gw2_ref, gb_ref, st_ref, hist_ref):
    npad = PAIR - N_META
    logg = jnp.concatenate(
        [jnp.zeros((npad, GLA_KW), F32), _log_gate(mg_ref[...], gw2_ref, gb_ref)], axis=0)
    row = lax.broadcasted_iota(jnp.int32, (PAIR, PAIR), 0)
    col = lax.broadcasted_iota(jnp.int32, (PAIR, PAIR), 1)
    suffix = jnp.where(col > row, 1.0, 0.0).astype(BF16)
    to_end = _rows_matmul_f32(suffix, logg)
    k_m = mp_ref[:, COL_K:COL_K + GLA_KW].astype(F32)
    k_end = (jnp.concatenate([jnp.zeros((npad, GLA_KW), F32), k_m], axis=0)
             * jnp.exp(to_end)).astype(BF16)
    v_m = mp_ref[:, COL_V:COL_V + GLA_WIDTH].astype(F32)
    v_t = jnp.concatenate([jnp.zeros((npad, GLA_WIDTH), F32), v_m], axis=0).T.astype(BF16)
    for h in range(GLA_HEADS):
        st_ref[h] = _dot(v_t[h * GLA_DV:(h + 1) * GLA_DV, :],
                         k_end[:, h * GLA_DK:(h + 1) * GLA_DK])
    hist_ref[...] = mp_ref[:, COL_PU:COL_PU + POOL_WIDTH]


def _proj_norm(x_ref, nw_ref, wg_ref, xn_ref, g_ref):
    x = x_ref[...]
    xn = (x * _rms_scale(x) * nw_ref[...]).astype(BF16)
    xn_ref[...] = xn
    g_ref[...] = _dot(xn, wg_ref[...])


def _proj_chunk(c, xn_ref, wm_ref, p_ref):
    cw = wm_ref.shape[2]
    p_ref[:, c * cw:(c + 1) * cw] = _dot(xn_ref[...], wm_ref[c]).astype(p_ref.dtype)


def _mix_block(p_ref, g_ref, gw2_ref, gb_ref, gnw_ref, pw_ref, ps_ref,
               st_ref, hist_ref, o_ref, fillers):
    tb = p_ref.shape[0]
    fillers = list(fillers)

    def fill():
        for emit in (fillers.pop(0) if fillers else ()):
            emit()

    g_raw = _gate_logits(g_ref[...], gw2_ref, gb_ref)
    fill()
    logg = _log_sigmoid(g_raw) * (1.0 / GATE_TAU)
    g_cum = _chunk_cumsum(logg)
    row = lax.broadcasted_iota(jnp.int32, (PAIR, PAIR), 0)
    col = lax.broadcasted_iota(jnp.int32, (PAIR, PAIR), 1)
    causal = ((row // CHUNK) == (col // CHUNK)) & (col <= row)
    fill()

    n_pairs = tb // PAIR
    units = [(p, h) for p in range(n_pairs) for h in range(GLA_HEADS)]
    rows = lambda p, c: slice(p * PAIR + c * CHUNK, p * PAIR + (c + 1) * CHUNK)
    slab = lambda p: slice(p * PAIR, (p + 1) * PAIR)
    ck = lambda h: slice(h * GLA_DK, (h + 1) * GLA_DK)
    cv = lambda h: slice(COL_V + h * GLA_DV, COL_V + (h + 1) * GLA_DV)
    cr = lambda h: slice(COL_R + h * GLA_DV, COL_R + (h + 1) * GLA_DV)

    q_dec, k_inv, k_end, dec = [], [], [], []
    for p in range(n_pairs):
        gc = g_cum[slab(p)]
        gl = [gc[CHUNK - 1:CHUNK], gc[PAIR - 1:PAIR]]
        gt = jnp.concatenate([jnp.broadcast_to(g, (CHUNK, GLA_KW)) for g in gl], axis=0)
        qf = p_ref[slab(p), COL_Q:COL_Q + GLA_KW].astype(F32)
        kf = p_ref[slab(p), COL_K:COL_K + GLA_KW].astype(F32)
        q_dec.append((qf * (GLA_DK ** -0.5) * jnp.exp(gc)).astype(BF16))
        k_inv.append((kf * jnp.exp(-gc)).astype(BF16))
        k_end.append((kf * jnp.exp(gt - gc)).astype(BF16))
        dec.append([jnp.exp(g) for g in gl])

    fill()
    scores = {u: _dot_nt(q_dec[u[0]][:, ck(u[1])], k_inv[u[0]][:, ck(u[1])]) for u in units}
    d_st = {(p, h, c): _dot_tn(p_ref[rows(p, c), cv(h)],
                               k_end[p][c * CHUNK:(c + 1) * CHUNK, ck(h)])
            for (p, h) in units for c in range(2)}

    fill()
    scores = {u: jnp.where(causal, scores[u], 0.0).astype(BF16) for u in units}
    st_in = {}
    for h in range(GLA_HEADS):
        st = st_ref[h]
        for p in range(n_pairs):
            for c in range(2):
                st_in[(p, h, c)] = st.astype(BF16)
                st = st * dec[p][c][:, ck(h)] + d_st[(p, h, c)]
        st_ref[h] = st

    fill()
    outs = {}
    for (p, h) in units:
        o = _dot(scores[(p, h)], p_ref[slab(p), cv(h)])
        outs[(p, h)] = o + jnp.concatenate(
            [_dot_nt(q_dec[p][c * CHUNK:(c + 1) * CHUNK, ck(h)], st_in[(p, h, c)])
             for c in range(2)], axis=0)

    n_groups = len(POOL_WINDOWS)
    cs = lambda g: slice(g * POOL_GC, (g + 1) * POOL_GC)
    cp = lambda g: slice(COL_PU + g * POOL_GC, COL_PU + (g + 1) * POOL_GC)
    y = []
    for g, w in enumerate(POOL_WINDOWS):
        cur = p_ref[:, cp(g)].astype(F32)
        acc = jnp.concatenate([hist_ref[:, cs(g)].astype(F32), cur], axis=0)
        k = 1
        while k < w:
            acc = acc + pltpu.roll(acc, k, axis=0)
            k *= 2
        y.append((acc[POOL_HIST:] * (1.0 / w) - cur).astype(BF16))
    hist_ref[...] = p_ref[tb - POOL_HIST:tb, COL_PU:COL_PU + POOL_WIDTH]
    y = [_dot(y[g], pw_ref[g]) for g in range(n_groups)]

    while fillers:
        fill()
    gnw = gnw_ref[...]
    for (p, h) in units:
        o = outs[(p, h)]
        o = o * _rms_scale(o) * gnw
        rf = p_ref[slab(p), cr(h)].astype(F32)
        gate = rf * (1.0 / (1.0 + jnp.exp(-rf)))
        o_ref[slab(p), h * GLA_DV:(h + 1) * GLA_DV] = (o * gate).astype(o_ref.dtype)
    for g in range(n_groups):
        o_ref[:, GLA_WIDTH + g * POOL_GC:GLA_WIDTH + (g + 1) * POOL_GC] = (
            y[g] * ps_ref[:, cs(g)]).astype(o_ref.dtype)


def _front_kernel(nt, x_ref, xnext_ref, nw_ref, wint_hbm, wout_hbm, meta_ref,
                  gw2_ref, gb_ref, gnw_ref, pw_ref, ps_ref,
                  w1_ref, w2_ref,
                  h_ref, w1b_ref, w2b_ref,
                  wm_ref, wg_ref, wo_ref, stage_ref, sem, mp_ref, mg_ref, st_ref, hist_ref,
                  xn_ref, p_ref, g_ref, m_ref):
    s = pl.program_id(0)

    @pl.when(s == 0)
    def _():
        _prepare_weights(wint_hbm, wout_hbm, stage_ref, sem, wm_ref, wg_ref, wo_ref)
        _proj_norm(meta_ref, nw_ref, wg_ref, xn_ref.at[pl.ds(0, N_META), :], mg_ref)
        for c in range(wm_ref.shape[0]):
            _proj_chunk(c, xn_ref.at[pl.ds(0, N_META), :], wm_ref, mp_ref)
        _proj_norm(x_ref, nw_ref, wg_ref, xn_ref, g_ref)

    @pl.when(lax.rem(s, nt) == 0)
    def _():
        _init_state(mp_ref, mg_ref, gw2_ref, gb_ref, st_ref, hist_ref)

    chunk = lambda c: functools.partial(_proj_chunk, c, xn_ref, wm_ref, p_ref)
    chunks = lambda lo, hi: [chunk(c) for c in range(lo // wm_ref.shape[2], hi // wm_ref.shape[2])]
    pool_mid = COL_PU + POOL_WIDTH // 2
    _mix_block(p_ref, g_ref, gw2_ref, gb_ref, gnw_ref, pw_ref, ps_ref,
               st_ref, hist_ref, m_ref,
               [chunks(COL_Q, COL_K), chunks(COL_K, COL_V), chunks(COL_V, COL_R),
                chunks(COL_PU, pool_mid), chunks(pool_mid, PROJ_WIDTH), chunks(COL_R, COL_PU)])
    w1b_ref[...] = w1_ref[...].astype(BF16)
    w2b_ref[...] = w2_ref[...].astype(BF16)
    h_ref[...] = x_ref[...] + _dot(m_ref[...], wo_ref[...])
    _proj_norm(xnext_ref, nw_ref, wg_ref, xn_ref, g_ref)


def _front(x2d, nw, w_in_t, w_out, meta, gw2, gb, gnw, pw, ps, w1, w2, batch, seq, tb):
    nt = seq // tb
    n_steps = batch * nt
    d = x2d.shape[1]
    assert w1.shape[0] % (8 * n_steps) == 0 and w2.shape[0] % (8 * n_steps) == 0
    assert w_in_t.shape == (PROJ_WIDTH + GATE_RANK, d) and PROJ_WIDTH % PREP_ROWS == 0
    assert all(c % PREP_ROWS == 0 for c in (COL_V, COL_R, COL_PU, w_out.shape[0]))
    full = lambda a: pl.BlockSpec(a.shape, lambda s: (0,) * a.ndim)
    return pl.pallas_call(
        functools.partial(_front_kernel, nt),
        grid=(batch * nt,),
        in_specs=[
            pl.BlockSpec((tb, d), lambda s: (s, 0)),
            pl.BlockSpec((tb, d), lambda s: (jnp.minimum(s + 1, batch * nt - 1), 0)),
            full(nw),
            pl.BlockSpec(memory_space=pl.ANY),
            pl.BlockSpec(memory_space=pl.ANY),
            full(meta), full(gw2), full(gb), full(gnw), full(pw), full(ps),
            pl.BlockSpec((w1.shape[0] // n_steps, w1.shape[1]), lambda s: (s, 0)),
            pl.BlockSpec((w2.shape[0] // n_steps, w2.shape[1]), lambda s: (s, 0)),
        ],
        out_specs=[
            pl.BlockSpec((tb, d), lambda s: (s, 0)),
            pl.BlockSpec((w1.shape[0] // n_steps, w1.shape[1]), lambda s: (s, 0)),
            pl.BlockSpec((w2.shape[0] // n_steps, w2.shape[1]), lambda s: (s, 0)),
        ],
        out_shape=[
            jax.ShapeDtypeStruct((batch * seq, d), F32),
            jax.ShapeDtypeStruct(w1.shape, BF16),
            jax.ShapeDtypeStruct(w2.shape, BF16),
        ],
        scratch_shapes=[
            pltpu.VMEM((PROJ_WIDTH // PREP_ROWS, d, PREP_ROWS), BF16),
            pltpu.VMEM((d, LANES), BF16),
            pltpu.VMEM(w_out.shape, BF16),
            pltpu.VMEM((PREP_ROWS // LANES, LANES, d), F32),
            pltpu.SemaphoreType.DMA((PREP_ROWS // LANES,)),
            pltpu.VMEM((N_META, PROJ_WIDTH), BF16),
            pltpu.VMEM((N_META, LANES), F32),
            pltpu.VMEM((GLA_HEADS, GLA_DV, GLA_DK), F32),
            pltpu.VMEM((POOL_HIST, POOL_WIDTH), BF16),
            pltpu.VMEM((tb, d), BF16),
            pltpu.VMEM((tb, PROJ_WIDTH), BF16),
            pltpu.VMEM((tb, LANES), F32),
            pltpu.VMEM((tb, MIX_WIDTH), BF16),
        ],
        compiler_params=pltpu.CompilerParams(
            dimension_semantics=("arbitrary",), vmem_limit_bytes=FRONT_VMEM_LIMIT),
        name="front",
    )(x2d, x2d, nw, w_in_t, w_out, meta, gw2, gb, gnw, pw, ps, w1, w2)


def _mlp_kernel(h_hbm, nw_ref, w1_ref, w2_ref, fnw_ref, o_ref, xn_ref, hbuf_ref, sem):
    m = pl.program_id(0)
    f = pl.program_id(1)
    tm = hbuf_ref.shape[0]

    def fetch(tile):
        return pltpu.make_async_copy(h_hbm.at[pl.ds(tile * tm, tm), :], hbuf_ref, sem)

    @pl.when((m == 0) & (f == 0))
    def _():
        fetch(0).start()

    @pl.when(f == 0)
    def _():
        fetch(m).wait()
        h = hbuf_ref[...]
        xn_ref[...] = (h * _rms_scale(h) * nw_ref[...]).astype(BF16)
        o_ref[...] = h

    @pl.when((f == pl.num_programs(1) // 2) & (m + 1 < pl.num_programs(0)))
    def _():
        fetch(m + 1).start()

    z = _dot(xn_ref[...], w1_ref[...])
    a = jnp.square(jnp.maximum(z, 0.0)).astype(BF16)
    o_ref[...] += _dot(a, w2_ref[...])

    @pl.when(f == pl.num_programs(1) - 1)
    def _():
        y = o_ref[...]
        o_ref[...] = y * _rms_scale(y) * fnw_ref[...]


def _mlp(h, nw, w1, w2, fnw, tm, tf):
    m, d = h.shape
    ff = w1.shape[1]
    return pl.pallas_call(
        _mlp_kernel,
        grid=(m // tm, ff // tf),
        in_specs=[
            pl.BlockSpec(memory_space=pl.ANY),
            pl.BlockSpec((1, d), lambda i, j: (0, 0)),
            pl.BlockSpec((d, tf), lambda i, j: (0, j)),
            pl.BlockSpec((tf, d), lambda i, j: (j, 0)),
            pl.BlockSpec((1, d), lambda i, j: (0, 0)),
        ],
        out_specs=pl.BlockSpec((tm, d), lambda i, j: (i, 0)),
        out_shape=jax.ShapeDtypeStruct((m, d), F32),
        scratch_shapes=[
            pltpu.VMEM((tm, d), BF16),
            pltpu.VMEM((tm, d), F32),
            pltpu.SemaphoreType.DMA(()),
        ],
        compiler_params=pltpu.CompilerParams(
            dimension_semantics=("arbitrary", "arbitrary"),
            vmem_limit_bytes=VMEM_LIMIT),
        name="mlp",
    )(h, nw, w1, w2, fnw)


def kernel(x, meta_tokens, norm1_w, w_in, gate_w2, gate_b, gla_norm_w, pool_w, pool_scale,
           w_out, norm2_w, mlp_w1, mlp_w2, final_norm_w):
    batch, seq, d = x.shape
    assert w_in.shape[0] == 1, "single-layer trunk"
    assert meta_tokens.shape[0] == N_META

    w_in_t = w_in[0].T
    gw2 = jnp.pad(gate_w2[0], ((0, LANES - GATE_RANK), (0, 0))).astype(BF16)
    gb = gate_b[0].reshape(1, GLA_KW)
    gnw = gla_norm_w[0].reshape(1, GLA_DV)
    pw = pool_w[0].astype(BF16)
    ps = pool_scale[0].reshape(1, POOL_WIDTH)
    nw1 = norm1_w[0].reshape(1, d)
    nw2 = norm2_w[0].reshape(1, d)
    fnw = final_norm_w.reshape(1, d)

    x2d = x.reshape(batch * seq, d)
    h1, w1b, w2b = _front(x2d, nw1, w_in_t, w_out[0], meta_tokens.astype(x.dtype), gw2, gb, gnw,
                          pw, ps, mlp_w1[0], mlp_w2[0], batch, seq, FRONT_TB)
    out = _mlp(h1, nw2, w1b, w2b, fnw, MLP_TM, MLP_TF)
    return out.reshape(batch, seq, d)
```

```python
import functools

import jax
import jax.numpy as jnp
from jax import lax
from jax.experimental import pallas as pl
from jax.experimental.pallas import tpu as pltpu

F32 = jnp.float32
BF16 = jnp.bfloat16

N_META = 16
CHUNK = 64
PAIR = 2 * CHUNK
GLA_HEADS = 4
GLA_DK = 128
GLA_DV = 256
GLA_KW = GLA_HEADS * GLA_DK
GLA_WIDTH = GLA_HEADS * GLA_DV
GATE_RANK = 16
GATE_TAU = 16.0
POOL_WINDOWS = (2, 4, 8, 16)
POOL_GC = 256
POOL_WIDTH = POOL_GC * len(POOL_WINDOWS)
POOL_HIST = 16
MIX_WIDTH = GLA_WIDTH + POOL_WIDTH
COL_Q, COL_K, COL_V = 0, GLA_KW, 2 * GLA_KW
COL_R = COL_V + GLA_WIDTH
COL_PU = COL_R + GLA_WIDTH
PROJ_WIDTH = COL_PU + POOL_WIDTH
EPS = 1e-6
LANES = 128
VMEM_CAPACITY = 64 * 1024 * 1024
VMEM_LIMIT = VMEM_CAPACITY - 8 * 1024 * 1024
FRONT_VMEM_LIMIT = VMEM_CAPACITY - 3 * 1024 * 1024

FRONT_TB = 256
PREP_ROWS = 512
MLP_TM, MLP_TF = 1024, 1024


def _rms_scale(x):
    return lax.rsqrt(jnp.mean(x * x, axis=-1, keepdims=True) + EPS)


def _dot(a, b):
    return jnp.dot(a, b, preferred_element_type=F32)


def _dot_nt(a, b):
    return lax.dot_general(a, b, (((1,), (1,)), ((), ())), preferred_element_type=F32)


def _dot_tn(a, b):
    return lax.dot_general(a, b, (((0,), (0,)), ((), ())), preferred_element_type=F32)


def _gate_logits(glr, gw2_ref, gb_ref):
    return _dot(glr.astype(BF16), gw2_ref[...]) + gb_ref[...]


def _log_sigmoid(g):
    return jnp.minimum(g, 0.0) - jnp.log(1.0 + jnp.exp(-jnp.abs(g)))


def _log_gate(glr, gw2_ref, gb_ref):
    return _log_sigmoid(_gate_logits(glr, gw2_ref, gb_ref)) * (1.0 / GATE_TAU)


def _chunk_cumsum(x):
    pos = lax.broadcasted_iota(jnp.int32, x.shape, 0) % CHUNK
    k = 1
    while k < CHUNK:
        x = x + jnp.where(pos >= k, pltpu.roll(x, k, axis=0), 0.0)
        k *= 2
    return x


def _rows_matmul_f32(sel_bf16, x):
    hi = x.astype(BF16)
    lo = (x - hi.astype(F32)).astype(BF16)
    return _dot(sel_bf16, hi) + _dot(sel_bf16, lo)


def _prepare_weights(wint_hbm, wout_hbm, stage_ref, sem, wm_ref, wg_ref, wo_ref):
    n_slots, rows = stage_ref.shape[0], stage_ref.shape[1]
    n_win_groups = wm_ref.shape[0]
    n_wout_groups = wout_hbm.shape[0] // (rows * n_slots)
    assert wm_ref.shape[2] == n_slots * rows and rows == LANES

    def copy(src_hbm, start, slot):
        return pltpu.make_async_copy(src_hbm.at[pl.ds(start, rows), :], stage_ref.at[slot],
                                     sem.at[slot])

    def win_start(piece):
        col = piece * rows
        return pl.multiple_of(col + jnp.where(col >= COL_PU, GATE_RANK, 0), GATE_RANK)

    def wout_start(piece):
        return pl.multiple_of(piece * rows, rows)

    for j in range(n_slots):
        copy(wint_hbm, win_start(j), j).start()

    def win_group(i, carry):
        for j in range(n_slots):
            copy(wint_hbm, 0, j).wait()
            wm_ref[i, :, j * rows:(j + 1) * rows] = stage_ref[j].T.astype(BF16)

            @pl.when(i + 1 < n_win_groups)
            def _():
                copy(wint_hbm, win_start(n_slots * (i + 1) + j), j).start()

            @pl.when(i + 1 == n_win_groups)
            def _():
                if j == 0:
                    copy(wint_hbm, COL_PU, 0).start()
                else:
                    copy(wout_hbm, wout_start(j - 1), j).start()
        return carry

    lax.fori_loop(0, n_win_groups, win_group, 0)
    copy(wint_hbm, 0, 0).wait()
    wg_ref[...] = stage_ref[0].T.astype(BF16)
    copy(wout_hbm, wout_start(n_slots - 1), 0).start()

    def wout_group(k, carry):
        for j in range(n_slots):
            slot = (j + 1) % n_slots
            piece = n_slots * k + j
            copy(wout_hbm, 0, slot).wait()
            wo_ref[pl.ds(wout_start(piece), rows), :] = stage_ref[slot].astype(BF16)

            @pl.when(k + 1 < n_wout_groups)
            def _():
                copy(wout_hbm, wout_start(piece + n_slots), slot).start()
        return carry

    lax.fori_loop(0, n_wout_groups, wout_group, 0)


def _init_state(mp_ref, mg_ref, gw2_ref, gb_ref, st_ref, hist_ref):
    npad = PAIR - N_META
    logg = jnp.concatenate(
        [jnp.zeros((npad, GLA_KW), F32), _log_gate(mg_ref[...], gw2_ref, gb_ref)], axis=0)
    row = lax.broadcasted_iota(jnp.int32, (PAIR, PAIR), 0)
    col = lax.broadcasted_iota(jnp.int32, (PAIR, PAIR), 1)
    suffix = jnp.where(col > row, 1.0, 0.0).astype(BF16)
    to_end = _rows_matmul_f32(suffix, logg)
    k_m = mp_ref[:, COL_K:COL_K + GLA_KW].astype(F32)
    k_end = (jnp.concatenate([jnp.zeros((npad, GLA_KW), F32), k_m], axis=0)
             * jnp.exp(to_end)).astype(BF16)
    v_m = mp_ref[:, COL_V:COL_V + GLA_WIDTH].astype(F32)
    v_t = jnp.concatenate([jnp.zeros((npad, GLA_WIDTH), F32), v_m], axis=0).T.astype(BF16)
    for h in range(GLA_HEADS):
        st_ref[h] = _dot(v_t[h * GLA_DV:(h + 1) * GLA_DV, :],
                         k_end[:, h * GLA_DK:(h + 1) * GLA_DK])
    hist_ref[...] = mp_ref[:, COL_PU:COL_PU + POOL_WIDTH]


def _proj_norm(x_ref, nw_ref, wg_ref, xn_ref, g_ref):
    x = x_ref[...]
    xn = (x * _rms_scale(x) * nw_ref[...]).astype(BF16)
    xn_ref[...] = xn
    g_ref[...] = _dot(xn, wg_ref[...])


def _proj_chunk(c, xn_ref, wm_ref, p_ref):
    cw = wm_ref.shape[2]
    p_ref[:, c * cw:(c + 1) * cw] = _dot(xn_ref[...], wm_ref[c]).astype(p_ref.dtype)


def _mix_block(p_ref, g_ref, gw2_ref, gb_ref, gnw_ref, pw_ref, ps_ref,
               st_ref, hist_ref, o_ref, fillers):
    tb = p_ref.shape[0]
    fillers = list(fillers)

    def fill():
        for emit in (fillers.pop(0) if fillers else ()):
            emit()

    g_raw = _gate_logits(g_ref[...], gw2_ref, gb_ref)
    fill()
    logg = _log_sigmoid(g_raw) * (1.0 / GATE_TAU)
    g_cum = _chunk_cumsum(logg)
    row = lax.broadcasted_iota(jnp.int32, (PAIR, PAIR), 0)
    col = lax.broadcasted_iota(jnp.int32, (PAIR, PAIR), 1)
    causal = ((row // CHUNK) == (col // CHUNK)) & (col <= row)
    fill()

    n_pairs = tb // PAIR
    units = [(p, h) for p in range(n_pairs) for h in range(GLA_HEADS)]
    rows = lambda p, c: slice(p * PAIR + c * CHUNK, p * PAIR + (c + 1) * CHUNK)
    slab = lambda p: slice(p * PAIR, (p + 1) * PAIR)
    ck = lambda h: slice(h * GLA_DK, (h + 1) * GLA_DK)
    cv = lambda h: slice(COL_V + h * GLA_DV, COL_V + (h + 1) * GLA_DV)
    cr = lambda h: slice(COL_R + h * GLA_DV, COL_R + (h + 1) * GLA_DV)

    q_dec, k_inv, k_end, dec = [], [], [], []
    for p in range(n_pairs):
        gc = g_cum[slab(p)]
        gl = [gc[CHUNK - 1:CHUNK], gc[PAIR - 1:PAIR]]
        gt = jnp.concatenate([jnp.broadcast_to(g, (CHUNK, GLA_KW)) for g in gl], axis=0)
        qf = p_ref[slab(p), COL_Q:COL_Q + GLA_KW].astype(F32)
        kf = p_ref[slab(p), COL_K:COL_K + GLA_KW].astype(F32)
        q_dec.append((qf * (GLA_DK ** -0.5) * jnp.exp(gc)).astype(BF16))
        k_inv.append((kf * jnp.exp(-gc)).T.astype(BF16))
        k_end.append((kf * jnp.exp(gt - gc)).astype(BF16))
        dec.append([jnp.exp(g) for g in gl])

    fill()
    scores = {u: _dot(q_dec[u[0]][:, ck(u[1])], k_inv[u[0]][ck(u[1]), :]) for u in units}
    d_st = {(p, h, c): _dot_tn(p_ref[rows(p, c), cv(h)],
                               k_end[p][c * CHUNK:(c + 1) * CHUNK, ck(h)])
            for (p, h) in units for c in range(2)}

    fill()
    scores = {u: jnp.where(causal, scores[u], 0.0).astype(BF16) for u in units}
    st_in = {}
    for h in range(GLA_HEADS):
        st = st_ref[h]
        for p in range(n_pairs):
            for c in range(2):
                st_in[(p, h, c)] = st.astype(BF16)
                st = st * dec[p][c][:, ck(h)] + d_st[(p, h, c)]
        st_ref[h] = st

    fill()
    outs = {}
    for (p, h) in units:
        o = _dot(scores[(p, h)], p_ref[slab(p), cv(h)])
        outs[(p, h)] = o + jnp.concatenate(
            [_dot_nt(q_dec[p][c * CHUNK:(c + 1) * CHUNK, ck(h)], st_in[(p, h, c)])
             for c in range(2)], axis=0)

    n_groups = len(POOL_WINDOWS)
    cs = lambda g: slice(g * POOL_GC, (g + 1) * POOL_GC)
    cp = lambda g: slice(COL_PU + g * POOL_GC, COL_PU + (g + 1) * POOL_GC)
    y = []
    for g, w in enumerate(POOL_WINDOWS):
        cur = p_ref[:, cp(g)].astype(F32)
        acc = jnp.concatenate([hist_ref[:, cs(g)].astype(F32), cur], axis=0)
        k = 1
        while k < w:
            acc = acc + pltpu.roll(acc, k, axis=0)
            k *= 2
        y.append((acc[POOL_HIST:] * (1.0 / w) - cur).astype(BF16))
    hist_ref[...] = p_ref[tb - POOL_HIST:tb, COL_PU:COL_PU + POOL_WIDTH]
    y = [_dot(y[g], pw_ref[g]) for g in range(n_groups)]

    while fillers:
        fill()
    gnw = gnw_ref[...]
    for (p, h) in units:
        o = outs[(p, h)]
        o = o * _rms_scale(o) * gnw
        rf = p_ref[slab(p), cr(h)].astype(F32)
        gate = rf * (1.0 / (1.0 + jnp.exp(-rf)))
        o_ref[slab(p), h * GLA_DV:(h + 1) * GLA_DV] = (o * gate).astype(o_ref.dtype)
    for g in range(n_groups):
        o_ref[:, GLA_WIDTH + g * POOL_GC:GLA_WIDTH + (g + 1) * POOL_GC] = (
            y[g] * ps_ref[:, cs(g)]).astype(o_ref.dtype)


def _front_kernel(nt, x_ref, xnext_ref, nw_ref, wint_hbm, wout_hbm, meta_ref,
                  gw2_ref, gb_ref, gnw_ref, pw_ref, ps_ref,
                  w1_ref, w2_ref,
                  h_ref, w1b_ref, w2b_ref,
                  wm_ref, wg_ref, wo_ref, stage_ref, sem, mp_ref, mg_ref, st_ref, hist_ref,
                  xn_ref, p_ref, g_ref, m_ref):
    s = pl.program_id(0)

    @pl.when(s == 0)
    def _():
        _prepare_weights(wint_hbm, wout_hbm, stage_ref, sem, wm_ref, wg_ref, wo_ref)
        _proj_norm(meta_ref, nw_ref, wg_ref, xn_ref.at[pl.ds(0, N_META), :], mg_ref)
        for c in range(wm_ref.shape[0]):
            _proj_chunk(c, xn_ref.at[pl.ds(0, N_META), :], wm_ref, mp_ref)
        _proj_norm(x_ref, nw_ref, wg_ref, xn_ref, g_ref)

    @pl.when(lax.rem(s, nt) == 0)
    def _():
        _init_state(mp_ref, mg_ref, gw2_ref, gb_ref, st_ref, hist_ref)

    chunk = lambda c: functools.partial(_proj_chunk, c, xn_ref, wm_ref, p_ref)
    chunks = lambda lo, hi: [chunk(c) for c in range(lo // wm_ref.shape[2], hi // wm_ref.shape[2])]
    pool_mid = COL_PU + POOL_WIDTH // 2
    _mix_block(p_ref, g_ref, gw2_ref, gb_ref, gnw_ref, pw_ref, ps_ref,
               st_ref, hist_ref, m_ref,
               [chunks(COL_Q, COL_K), chunks(COL_K, COL_V), chunks(COL_V, COL_R),
                chunks(COL_PU, pool_mid), chunks(pool_mid, PROJ_WIDTH), chunks(COL_R, COL_PU)])
    w1b_ref[...] = w1_ref[...].astype(BF16)
    w2b_ref[...] = w2_ref[...].astype(BF16)
    h_ref[...] = x_ref[...] + _dot(m_ref[...], wo_ref[...])
    _proj_norm(xnext_ref, nw_ref, wg_ref, xn_ref, g_ref)


def _front(x2d, nw, w_in_t, w_out, meta, gw2, gb, gnw, pw, ps, w1, w2, batch, seq, tb):
    nt = seq // tb
    n_steps = batch * nt
    d = x2d.shape[1]
    assert w1.shape[0] % (8 * n_steps) == 0 and w2.shape[0] % (8 * n_steps) == 0
    assert w_in_t.shape == (PROJ_WIDTH + GATE_RANK, d) and PROJ_WIDTH % PREP_ROWS == 0
    assert all(c % PREP_ROWS == 0 for c in (COL_V, COL_R, COL_PU, w_out.shape[0]))
    full = lambda a: pl.BlockSpec(a.shape, lambda s: (0,) * a.ndim)
    return pl.pallas_call(
        functools.partial(_front_kernel, nt),
        grid=(batch * nt,),
        in_specs=[
            pl.BlockSpec((tb, d), lambda s: (s, 0)),
            pl.BlockSpec((tb, d), lambda s: (jnp.minimum(s + 1, batch * nt - 1), 0)),
            full(nw),
            pl.BlockSpec(memory_space=pl.ANY),
            pl.BlockSpec(memory_space=pl.ANY),
            full(meta), full(gw2), full(gb), full(gnw), full(pw), full(ps),
            pl.BlockSpec((w1.shape[0] // n_steps, w1.shape[1]), lambda s: (s, 0)),
            pl.BlockSpec((w2.shape[0] // n_steps, w2.shape[1]), lambda s: (s, 0)),
        ],
        out_specs=[
            pl.BlockSpec((tb, d), lambda s: (s, 0)),
            pl.BlockSpec((w1.shape[0] // n_steps, w1.shape[1]), lambda s: (s, 0)),
            pl.BlockSpec((w2.shape[0] // n_steps, w2.shape[1]), lambda s: (s, 0)),
        ],
        out_shape=[
            jax.ShapeDtypeStruct((batch * seq, d), F32),
            jax.ShapeDtypeStruct(w1.shape, BF16),
            jax.ShapeDtypeStruct(w2.shape, BF16),
        ],
        scratch_shapes=[
            pltpu.VMEM((PROJ_WIDTH // PREP_ROWS, d, PREP_ROWS), BF16),
            pltpu.VMEM((d, LANES), BF16),
            pltpu.VMEM(w_out.shape, BF16),
            pltpu.VMEM((PREP_ROWS // LANES, LANES, d), F32),
            pltpu.SemaphoreType.DMA((PREP_ROWS // LANES,)),
            pltpu.VMEM((N_META, PROJ_WIDTH), BF16),
            pltpu.VMEM((N_META, LANES), F32),
            pltpu.VMEM((GLA_HEADS, GLA_DV, GLA_DK), F32),
            pltpu.VMEM((POOL_HIST, POOL_WIDTH), BF16),
            pltpu.VMEM((tb, d), BF16),
            pltpu.VMEM((tb, PROJ_WIDTH), BF16),
            pltpu.VMEM((tb, LANES), F32),
            pltpu.VMEM((tb, MIX_WIDTH), BF16),
        ],
        compiler_params=pltpu.CompilerParams(
            dimension_semantics=("arbitrary",), vmem_limit_bytes=FRONT_VMEM_LIMIT),
        name="front",
    )(x2d, x2d, nw, w_in_t, w_out, meta, gw2, gb, gnw, pw, ps, w1, w2)


def _mlp_kernel(h_hbm, nw_ref, w1_ref, w2_ref, fnw_ref, o_ref, xn_ref, hbuf_ref, sem):
    m = pl.program_id(0)
    f = pl.program_id(1)
    tm = hbuf_ref.shape[0]

    def fetch(tile):
        return pltpu.make_async_copy(h_hbm.at[pl.ds(tile * tm, tm), :], hbuf_ref, sem)

    @pl.when((m == 0) & (f == 0))
    def _():
        fetch(0).start()

    @pl.when(f == 0)
    def _():
        fetch(m).wait()
        h = hbuf_ref[...]
        xn_ref[...] = (h * _rms_scale(h) * nw_ref[...]).astype(BF16)
        o_ref[...] = h

    @pl.when((f == pl.num_programs(1) // 2) & (m + 1 < pl.num_programs(0)))
    def _():
        fetch(m + 1).start()

    z = _dot(xn_ref[...], w1_ref[...])
    a = jnp.square(jnp.maximum(z, 0.0)).astype(BF16)
    o_ref[...] += _dot(a, w2_ref[...])

    @pl.when(f == pl.num_programs(1) - 1)
    def _():
        y = o_ref[...]
        o_ref[...] = y * _rms_scale(y) * fnw_ref[...]


def _mlp(h, nw, w1, w2, fnw, tm, tf):
    m, d = h.shape
    ff = w1.shape[1]
    return pl.pallas_call(
        _mlp_kernel,
        grid=(m // tm, ff // tf),
        in_specs=[
            pl.BlockSpec(memory_space=pl.ANY),
            pl.BlockSpec((1, d), lambda i, j: (0, 0)),
            pl.BlockSpec((d, tf), lambda i, j: (0, j)),
            pl.BlockSpec((tf, d), lambda i, j: (j, 0)),
            pl.BlockSpec((1, d), lambda i, j: (0, 0)),
        ],
        out_specs=pl.BlockSpec((tm, d), lambda i, j: (i, 0)),
        out_shape=jax.ShapeDtypeStruct((m, d), F32),
        scratch_shapes=[
            pltpu.VMEM((tm, d), BF16),
            pltpu.VMEM((tm, d), F32),
            pltpu.SemaphoreType.DMA(()),
        ],
        compiler_params=pltpu.CompilerParams(
            dimension_semantics=("arbitrary", "arbitrary"),
            vmem_limit_bytes=VMEM_LIMIT),
        name="mlp",
    )(h, nw, w1, w2, fnw)


def kernel(x, meta_tokens, norm1_w, w_in, gate_w2, gate_b, gla_norm_w, pool_w, pool_scale,
           w_out, norm2_w, mlp_w1, mlp_w2, final_norm_w):
    batch, seq, d = x.shape
    assert w_in.shape[0] == 1, "single-layer trunk"
    assert meta_tokens.shape[0] == N_META

    w_in_t = w_in[0].T
    gw2 = jnp.pad(gate_w2[0], ((0, LANES - GATE_RANK), (0, 0))).astype(BF16)
    gb = gate_b[0].reshape(1, GLA_KW)
    gnw = gla_norm_w[0].reshape(1, GLA_DV)
    pw = pool_w[0].astype(BF16)
    ps = pool_scale[0].reshape(1, POOL_WIDTH)
    nw1 = norm1_w[0].reshape(1, d)
    nw2 = norm2_w[0].reshape(1, d)
    fnw = final_norm_w.reshape(1, d)

    x2d = x.reshape(batch * seq, d)
    h1, w1b, w2b = _front(x2d, nw1, w_in_t, w_out[0], meta_tokens.astype(x.dtype), gw2, gb, gnw,
                          pw, ps, mlp_w1[0], mlp_w2[0], batch, seq, FRONT_TB)
    out = _mlp(h1, nw2, w1b, w2b, fnw, MLP_TM, MLP_TF)
    return out.reshape(batch, seq, d)
```
